```python
import math
import jax, jax.numpy as jnp
from jax import lax
import numpy as np

D_MODEL = 2048
BATCH = 2
SEQ = 16384
DEPTH = 1
DEC_BATCH = 16
DEC_SEQ = 32
PAST_LEN = 2048

CHUNK = 64
Q_BLOCK = 128
HA = 8
HD_A = 64
VD_A = 2 * HD_A
HB = 8
HD_B = 128
D_FF = 4 * D_MODEL
ROPE_THETA = 10000.0
EPS = 1e-6
NEG = -1e30

QA_W = 2 * HA * HD_A
KA_W = 2 * HA * HD_A
VA_W = HA * VD_A
QB_W = HB * HD_B
KB_W = HB * HD_B
VB_W = HB * HD_B
IN_W = QA_W + KA_W + VA_W + QB_W + KB_W + VB_W
SPLITS = (QA_W, QA_W + KA_W, QA_W + KA_W + VA_W, QA_W + KA_W + VA_W + QB_W,
          QA_W + KA_W + VA_W + QB_W + KB_W)

kernel_name = 'hybrid_diffattn_stickbreak_stream_step'


def rmsnorm(x, g):
    xf = x.astype(jnp.float32)
    y = xf * lax.rsqrt(jnp.mean(xf * xf, axis=-1, keepdims=True) + EPS)
    return (y * g.astype(jnp.float32)).astype(x.dtype)


def rope(x, pos):
    half = x.shape[-1] // 2
    inv = ROPE_THETA ** (-jnp.arange(half, dtype=jnp.float32) / half)
    ang = pos.astype(jnp.float32)[:, None] * inv[None, :]
    cos = jnp.cos(ang)[None, :, None, :]
    sin = jnp.sin(ang)[None, :, None, :]
    xf = x.astype(jnp.float32)
    x1, x2 = xf[..., :half], xf[..., half:]
    return jnp.concatenate([x1 * cos - x2 * sin, x2 * cos + x1 * sin], axis=-1).astype(x.dtype)


def project(h, pos, w_in):
    b, t, _ = h.shape
    z = h @ w_in
    qa, ka, va, qb, kb, vb = jnp.split(z, SPLITS, axis=-1)
    qa = rope(qa.reshape(b, t, 2 * HA, HD_A), pos)
    ka = rope(ka.reshape(b, t, 2 * HA, HD_A), pos)
    va = va.reshape(b, t, HA, VD_A)
    qb = qb.reshape(b, t, HB, HD_B)
    kb = kb.reshape(b, t, HB, HD_B)
    vb = vb.reshape(b, t, HB, HD_B)
    return qa, ka, va, qb, kb, vb


def diff_attend(qa, ka, va, q_pos, k_pos, lam, g_head, lam_init):
    b, tq = qa.shape[0], qa.shape[1]
    s = jnp.einsum('bqhd,bkhd->bhqk', qa, ka, preferred_element_type=jnp.float32) * (HD_A ** -0.5)
    limit = (q_pos // CHUNK + 1) * CHUNK
    mask = k_pos[None, :] < limit[:, None]
    p = jax.nn.softmax(jnp.where(mask, s, NEG), axis=-1)
    w = p[:, :HA] - lam * p[:, HA:]
    o = jnp.einsum('bhqk,bkhe->bqhe', w, va, preferred_element_type=jnp.float32)
    o = o * lax.rsqrt(jnp.mean(o * o, axis=-1, keepdims=True) + EPS)
    o = o * g_head.astype(jnp.float32) * (1.0 - lam_init)
    return o.reshape(b, tq, HA * VD_A).astype(qa.dtype)


def sb_attend(qb, kb, vb, q_pos, k_pos):
    b, tq = qb.shape[0], qb.shape[1]
    z = jnp.einsum('bqhd,bkhd->bhqk', qb, kb, preferred_element_type=jnp.float32) * (HD_B ** -0.5)
    mask = k_pos[None, :] < q_pos[:, None]
    log_keep = jnp.where(mask, jax.nn.log_sigmoid(-z), 0.0)
    suffix = lax.cumsum(log_keep, axis=3, reverse=True) - log_keep
    a = jnp.where(mask, jnp.exp(jax.nn.log_sigmoid(z) + suffix), 0.0)
    o = jnp.einsum('bhqk,bkhe->bqhe', a, vb, preferred_element_type=jnp.float32)
    return o.reshape(b, tq, HB * HD_B).astype(qb.dtype)


def merge_and_mlp(x, h, oa, ob, w_gate, w_proj_a, w_proj_b, w_out, g_post_mix,
                  g_pre_mlp, w_up, w_down, g_post_mlp):
    g = jax.nn.sigmoid((h @ w_gate).astype(jnp.float32))
    ya = (oa @ w_proj_a).astype(jnp.float32)
    yb = (ob @ w_proj_b).astype(jnp.float32)
    merged = (g[..., :D_MODEL] * ya + g[..., D_MODEL:] * yb).astype(x.dtype)
    x = x + rmsnorm(merged @ w_out, g_post_mix)
    hm = rmsnorm(x, g_pre_mlp)
    u = jnp.square(jax.nn.relu(hm @ w_up))
    return x + rmsnorm(u @ w_down, g_post_mlp)


def setup_inputs(seed: int = 0) -> dict:
    key = jax.random.key(seed)
    ks = jax.random.split(key, 32)
    f32 = jnp.float32
    nrm = lambda k, shape, scale: jax.random.normal(k, shape, f32) * scale
    gain = lambda k, n: 1.0 + 0.05 * jax.random.normal(k, (DEPTH, n), f32)
    return {
        'x_prompt': nrm(ks[0], (BATCH, SEQ, D_MODEL), 1.0),
        'x_sample': nrm(ks[1], (DEC_BATCH, DEC_SEQ, D_MODEL), 1.0),
        'cache_diff_k': nrm(ks[2], (DEPTH, DEC_BATCH, PAST_LEN, 2 * HA, HD_A), 1.0),
        'cache_diff_v': nrm(ks[3], (DEPTH, DEC_BATCH, PAST_LEN, HA, VD_A), 1.0),
        'cache_sb_k': nrm(ks[4], (DEPTH, DEC_BATCH, PAST_LEN, HB, HD_B), 1.0),
        'cache_sb_v': nrm(ks[5], (DEPTH, DEC_BATCH, PAST_LEN, HB, HD_B), 1.0),
        'g_pre_mix': gain(ks[6], D_MODEL),
        'w_in': nrm(ks[7], (DEPTH, D_MODEL, IN_W), D_MODEL ** -0.5),
        'lambda_q1': nrm(ks[8], (DEPTH, HD_A), 0.1),
        'lambda_k1': nrm(ks[9], (DEPTH, HD_A), 0.1),
        'lambda_q2': nrm(ks[10], (DEPTH, HD_A), 0.1),
        'lambda_k2': nrm(ks[11], (DEPTH, HD_A), 0.1),
        'g_diff_head': gain(ks[12], VD_A),
        'w_gate': nrm(ks[13], (DEPTH, D_MODEL, 2 * D_MODEL), D_MODEL ** -0.5),
        'w_proj_a': nrm(ks[14], (DEPTH, VA_W, D_MODEL), VA_W ** -0.5),
        'w_proj_b': nrm(ks[15], (DEPTH, VB_W, D_MODEL), VB_W ** -0.5),
        'w_out': nrm(ks[16], (DEPTH, D_MODEL, D_MODEL), D_MODEL ** -0.5),
        'g_post_mix': gain(ks[17], D_MODEL),
        'g_pre_mlp': gain(ks[18], D_MODEL),
        'w_up': nrm(ks[19], (DEPTH, D_MODEL, D_FF), D_MODEL ** -0.5),
        'w_down': nrm(ks[20], (DEPTH, D_FF, D_MODEL), D_FF ** -0.5),
        'g_post_mlp': gain(ks[21], D_MODEL),
    }


def reference(x_prompt, x_sample, cache_diff_k, cache_diff_v, cache_sb_k, cache_sb_v,
              g_pre_mix, w_in, lambda_q1, lambda_k1, lambda_q2, lambda_k2, g_diff_head,
              w_gate, w_proj_a, w_proj_b, w_out, g_post_mix, g_pre_mlp, w_up, w_down,
              g_post_mlp):
    bp, tp, _ = x_prompt.shape
    bs, ts, _ = x_sample.shape
    past = cache_diff_k.shape[2]
    nblk = tp // Q_BLOCK
    pos_p = jnp.arange(tp, dtype=jnp.int32)
    pos_s = past + jnp.arange(ts, dtype=jnp.int32)
    kpos_s = jnp.arange(past + ts, dtype=jnp.int32)

    xp, xs = x_prompt, x_sample
    dk_p, dv_p, sk_p, sv_p = [], [], [], []
    dk_s, dv_s, sk_s, sv_s = [], [], [], []
    for l in range(DEPTH):
        lam_init = 0.8 - 0.6 * math.exp(-0.3 * l)
        lam = (jnp.exp(jnp.sum(lambda_q1[l].astype(jnp.float32) * lambda_k1[l].astype(jnp.float32)))
               - jnp.exp(jnp.sum(lambda_q2[l].astype(jnp.float32) * lambda_k2[l].astype(jnp.float32)))
               + lam_init)

        hp = rmsnorm(xp, g_pre_mix[l])
        qa, ka, va, qb, kb, vb = project(hp, pos_p, w_in[l])
        qa_blk = qa.reshape(bp, nblk, Q_BLOCK, 2 * HA, HD_A).swapaxes(0, 1)
        qb_blk = qb.reshape(bp, nblk, Q_BLOCK, HB, HD_B).swapaxes(0, 1)
        qpos_blk = pos_p.reshape(nblk, Q_BLOCK)

        def block(args, ka=ka, va=va, kb=kb, vb=vb, lam=lam, l=l, lam_init=lam_init):
            qa_b, qb_b, qp = args
            return (diff_attend(qa_b, ka, va, qp, pos_p, lam, g_diff_head[l], lam_init),
                    sb_attend(qb_b, kb, vb, qp, pos_p))

        oa, ob = lax.map(block, (qa_blk, qb_blk, qpos_blk))
        oa = oa.swapaxes(0, 1).reshape(bp, tp, VA_W)
        ob = ob.swapaxes(0, 1).reshape(bp, tp, VB_W)
        xp = merge_and_mlp(xp, hp, oa, ob, w_gate[l], w_proj_a[l], w_proj_b[l], w_out[l],
                           g_post_mix[l], g_pre_mlp[l], w_up[l], w_down[l], g_post_mlp[l])
        dk_p.append(ka); dv_p.append(va); sk_p.append(kb); sv_p.append(vb)

        hs = rmsnorm(xs, g_pre_mix[l])
        qa2, ka2, va2, qb2, kb2, vb2 = project(hs, pos_s, w_in[l])
        ka_all = jnp.concatenate([cache_diff_k[l], ka2], axis=1)
        va_all = jnp.concatenate([cache_diff_v[l], va2], axis=1)
        kb_all = jnp.concatenate([cache_sb_k[l], kb2], axis=1)
        vb_all = jnp.concatenate([cache_sb_v[l], vb2], axis=1)
        oa2 = diff_attend(qa2, ka_all, va_all, pos_s, kpos_s, lam, g_diff_head[l], lam_init)
        ob2 = sb_attend(qb2, kb_all, vb_all, pos_s, kpos_s)
        xs = merge_and_mlp(xs, hs, oa2, ob2, w_gate[l], w_proj_a[l], w_proj_b[l], w_out[l],
                           g_post_mix[l], g_pre_mlp[l], w_up[l], w_down[l], g_post_mlp[l])
        dk_s.append(ka2); dv_s.append(va2); sk_s.append(kb2); sv_s.append(vb2)

    new_diff_k_prompt = jnp.stack(dk_p)
    new_diff_v_prompt = jnp.stack(dv_p)
    new_sb_k_prompt = jnp.stack(sk_p)
    new_sb_v_prompt = jnp.stack(sv_p)
    new_diff_k_sample = jnp.stack(dk_s)
    new_diff_v_sample = jnp.stack(dv_s)
    new_sb_k_sample = jnp.stack(sk_s)
    new_sb_v_sample = jnp.stack(sv_s)
    return (xp, xs, new_diff_k_prompt, new_diff_v_prompt, new_sb_k_prompt, new_sb_v_prompt,
            new_diff_k_sample, new_diff_v_sample, new_sb_k_sample, new_sb_v_sample)
```

```python
import functools
import math

import jax
import jax.numpy as jnp
from jax import lax
from jax.experimental import pallas as pl
from jax.experimental.pallas import tpu as pltpu

F32 = jnp.float32
BF16 = jnp.bfloat16

CHUNK = 64
HA = 8
HD_A = 64
VD_A = 128
HB = 8
HD_B = 128
SEG_W = 1024
LANES = 128
N_GROUPS = SEG_W // LANES
ROPE_THETA = 10000.0
EPS = 1e-6
NEG = -1e30
LAM_INIT = 0.8 - 0.6 * math.exp(-0.3 * 0)
UNDERFLOW_LOG = -104.0
VMEM_LIMIT = 56 * 1024 * 1024


def _tile(n, pref, mult=8):
    if n <= pref:
        return n
    for t in range(pref, 0, -1):
        if n % t == 0 and t % mult == 0:
            return t
    raise ValueError(f"no tile for {n}")


def _params(*sem):
    return pltpu.CompilerParams(dimension_semantics=sem, vmem_limit_bytes=VMEM_LIMIT)


def _dot(a, b):
    return jnp.dot(a, b, preferred_element_type=F32)


def _dot_nt(a, b):
    return lax.dot_general(a, b, (((1,), (1,)), ((), ())), preferred_element_type=F32)


def _rms(x, g):
    return x * lax.rsqrt(jnp.mean(x * x, axis=-1, keepdims=True) + EPS) * g


def _prenorm_kernel(x_ref, g_ref, h_ref):
    h_ref[...] = _rms(x_ref[...], g_ref[...]).astype(BF16)


def _prenorm(x, g):
    rows, d = x.shape
    tm = _tile(rows, 512)
    return pl.pallas_call(
        _prenorm_kernel,
        grid=(rows // tm,),
        in_specs=[pl.BlockSpec((tm, d), lambda i: (i, 0)), pl.BlockSpec((1, d), lambda i: (0, 0))],
        out_specs=pl.BlockSpec((tm, d), lambda i: (i, 0)),
        out_shape=jax.ShapeDtypeStruct((rows, d), BF16),
        compiler_params=_params("parallel"),
        name="prenorm",
    )(x, g)


def _store_groups(t_ref, z):
    for c in range(N_GROUPS):
        t_ref[c] = z[:, c * LANES:(c + 1) * LANES].astype(BF16)


def _rope_groups(z, cos, sin):
    lane = lax.broadcasted_iota(jnp.int32, (1, LANES), 1)
    first_half = (lane % HD_A) < (HD_A // 2)
    out = []
    for c in range(N_GROUPS):
        zc = z[:, c * LANES:(c + 1) * LANES]
        partner = jnp.where(first_half, pltpu.roll(zc, LANES - HD_A // 2, 1), pltpu.roll(zc, HD_A // 2, 1))
        out.append(zc * cos + partner * sin)
    return out


def _proj_kv_kernel(h_ref, w_ref, f_ref, t_ref):
    z = _dot(h_ref[...], w_ref[...])
    f_ref[...] = z
    _store_groups(t_ref, z)


def _proj_q_kernel(h_ref, w_ref, t_ref):
    _store_groups(t_ref, _dot(h_ref[...], w_ref[...]))


def _proj_k_rope_kernel(h_ref, w_ref, cos_ref, sin_ref, f_ref, t_ref):
    z = _dot(h_ref[...], w_ref[...])
    for c, zc in enumerate(_rope_groups(z, cos_ref[...], sin_ref[...])):
        f_ref[:, c * LANES:(c + 1) * LANES] = zc
        t_ref[c] = zc.astype(BF16)


def _proj_q_rope_kernel(h_ref, w_ref, cos_ref, sin_ref, t_ref):
    z = _dot(h_ref[...], w_ref[...])
    for c, zc in enumerate(_rope_groups(z, cos_ref[...], sin_ref[...])):
        t_ref[c] = (zc * (HD_A ** -0.5)).astype(BF16)


def _project(h, w, mode, cos=None, sin=None):
    rows, d = h.shape
    tm = _tile(rows, 512)
    n_tab = None if cos is None else cos.shape[0] // tm
    in_specs = [pl.BlockSpec((tm, d), lambda i: (i, 0)), pl.BlockSpec((d, SEG_W), lambda i: (0, 0))]
    args = [h, w]
    if cos is not None:
        tab = pl.BlockSpec((tm, LANES), lambda i: (i % n_tab, 0))
        in_specs += [tab, tab]
        args += [cos, sin]
    f_spec = pl.BlockSpec((tm, SEG_W), lambda i: (i, 0))
    t_spec = pl.BlockSpec((N_GROUPS, tm, LANES), lambda i: (0, i, 0))
    f_shape = jax.ShapeDtypeStruct((rows, SEG_W), F32)
    t_shape = jax.ShapeDtypeStruct((N_GROUPS, rows, LANES), BF16)
    with_f32 = mode in ("kv", "k_rope")
    body = {"kv": _proj_kv_kernel, "q": _proj_q_kernel, "k_rope": _proj_k_rope_kernel,
            "q_rope": _proj_q_rope_kernel}[mode]
    out = pl.pallas_call(
        body,
        grid=(rows // tm,),
        in_specs=in_specs,
        out_specs=[f_spec, t_spec] if with_f32 else t_spec,
        out_shape=[f_shape, t_shape] if with_f32 else t_shape,
        compiler_params=_params("parallel"),
        name="proj_" + mode,
    )(*args)
    return (out[0], out[1]) if with_f32 else (None, out)


def _rope_tables(pos):
    half = HD_A // 2
    inv = ROPE_THETA ** (-jnp.arange(half, dtype=F32) / half)
    ang = pos.astype(F32)[:, None] * inv[None, :]
    cos, sin = jnp.cos(ang), jnp.sin(ang)
    reps = LANES // HD_A
    return (jnp.tile(jnp.concatenate([cos, cos], axis=-1), (1, reps)),
            jnp.tile(jnp.concatenate([-sin, sin], axis=-1), (1, reps)))


def _head_lanes(q, c):
    lane = lax.broadcasted_iota(jnp.int32, (1, LANES), 1)
    return jnp.where((lane >= c * HD_A) & (lane < (c + 1) * HD_A), q, jnp.zeros_like(q))


def _softmax_block(s, state, v, acc_ref):
    m, l = state
    m_new = jnp.maximum(m, jnp.max(s, axis=-1, keepdims=True))
    alpha = jnp.exp(m - m_new)
    p = jnp.exp(s - m_new)
    l_new = alpha * l + jnp.sum(p, axis=-1, keepdims=True)
    acc_ref[...] = alpha * acc_ref[...] + _dot(p.astype(BF16), v)
    return m_new, l_new


def _suffix_matrix(n):
    j = lax.broadcasted_iota(jnp.int32, (n, n), 0)
    s = lax.broadcasted_iota(jnp.int32, (n, n), 1)
    return jnp.where(j >= s, 1.0, 0.0).astype(BF16)


def _stick_block(z, mask, carry, v, acc_ref, suffix_mat):
    log_keep = jnp.minimum(-z, 0.0) - jnp.log(1.0 + jnp.exp(-jnp.abs(z)))
    if mask is not None:
        log_keep = jnp.where(mask, log_keep, 0.0)
    hi = log_keep.astype(BF16)
    lo = (log_keep - hi.astype(F32)).astype(BF16)
    suffix = _dot(hi, suffix_mat) + _dot(lo, suffix_mat)
    a = jnp.exp(z + suffix + carry)
    if mask is not None:
        a = jnp.where(mask, a, 0.0)
    acc_ref[...] += _dot(a.astype(BF16), v)
    return carry + suffix[:, 0:1]


def _lambda(lam_ref):
    lam = lam_ref[...]
    d1 = jnp.sum(lam[0:1] * lam[1:2], axis=-1, keepdims=True)
    d2 = jnp.sum(lam[2:3] * lam[3:4], axis=-1, keepdims=True)
    return jnp.exp(d1) - jnp.exp(d2) + LAM_INIT


def _diff_finish(o1, o2, lam, g_head):
    o = o1 - lam * o2
    o = o * lax.rsqrt(jnp.mean(o * o, axis=-1, keepdims=True) + EPS)
    return (o * g_head * (1.0 - LAM_INIT)).astype(BF16)


def _diff_prompt_kernel(q1_ref, q2_ref, k1_ref, k2_ref, v_ref, lam_ref, g_ref, o_ref, acc_ref, *, tq, tk):
    qi = pl.program_id(2)
    lam = _lambda(lam_ref)
    n_full = (qi * tq) // tk
    q_pos = qi * tq + lax.broadcasted_iota(jnp.int32, (tq, 1), 0)
    limit = (q_pos // CHUNK + 1) * CHUNK
    k_pos = n_full * tk + lax.broadcasted_iota(jnp.int32, (1, tk), 1)
    edge_mask = k_pos < limit

    for c in range(2):
        outs = []
        for q_ref, k_ref in ((q1_ref, k1_ref), (q2_ref, k2_ref)):
            q = _head_lanes(q_ref[0], c)
            acc_ref[...] = jnp.zeros_like(acc_ref)

            def body(j, state, q=q, k_ref=k_ref, c=c):
                start = pl.multiple_of(j * tk, tk)
                s = _dot_nt(q, k_ref[0, pl.ds(start, tk), :])
                return _softmax_block(s, state, v_ref[c, pl.ds(start, tk), :], acc_ref)

            state = (jnp.full((tq, 1), NEG, F32), jnp.zeros((tq, 1), F32))
            state = lax.fori_loop(0, n_full, body, state)
            start = pl.multiple_of(n_full * tk, tk)
            s = jnp.where(edge_mask, _dot_nt(q, k_ref[0, pl.ds(start, tk), :]), NEG)
            _, l = _softmax_block(s, state, v_ref[c, pl.ds(start, tk), :], acc_ref)
            outs.append(acc_ref[...] / l)
        o_ref[:, c * VD_A:(c + 1) * VD_A] = _diff_finish(outs[0], outs[1], lam, g_ref[...])


def _diff_prompt(qa_t, ka_t, va_t, lam_vecs, g_head, batch, seq):
    tq = _tile(seq, 512, CHUNK)
    tk = tq
    nq = seq // tq
    half = N_GROUPS // 2
    q_spec = lambda off: pl.BlockSpec((1, tq, LANES), lambda b, p, i: (p + off, b * nq + i, 0))
    k_spec = lambda off: pl.BlockSpec((1, seq, LANES), lambda b, p, i: (p + off, b, 0))
    return pl.pallas_call(
        functools.partial(_diff_prompt_kernel, tq=tq, tk=tk),
        grid=(batch, half, nq),
        in_specs=[q_spec(0), q_spec(half), k_spec(0), k_spec(half),
                  pl.BlockSpec((2, seq, LANES), lambda b, p, i: (p, b, 0)),
                  pl.BlockSpec((4, HD_A), lambda b, p, i: (0, 0)),
                  pl.BlockSpec((1, VD_A), lambda b, p, i: (0, 0))],
        out_specs=pl.BlockSpec((tq, 2 * VD_A), lambda b, p, i: (b * nq + i, p)),
        out_shape=jax.ShapeDtypeStruct((batch * seq, SEG_W), BF16),
        scratch_shapes=[pltpu.VMEM((tq, VD_A), F32)],
        compiler_params=_params("parallel", "parallel", "arbitrary"),
        name="diff_prompt",
    )(qa_t, qa_t, ka_t, ka_t, va_t, lam_vecs, g_head)


def _stick_prompt_kernel(q_ref, k_ref, v_ref, o_ref, acc_ref, *, tq, tk):
    qi = pl.program_id(2)
    q = q_ref[0]
    suffix_mat = _suffix_matrix(tk)
    scale = HD_B ** -0.5
    acc_ref[...] = jnp.zeros_like(acc_ref)
    carry = jnp.zeros((tq, 1), F32)
    q_pos = qi * tq + lax.broadcasted_iota(jnp.int32, (tq, 1), 0)

    n_edge = tq // tk
    for e in range(n_edge - 1, -1, -1):
        start = pl.multiple_of(qi * tq + e * tk, tk)
        k_pos = start + lax.broadcasted_iota(jnp.int32, (1, tk), 1)
        z = _dot_nt(q, k_ref[0, pl.ds(start, tk), :]) * scale
        carry = _stick_block(z, k_pos < q_pos, carry, v_ref[0, pl.ds(start, tk), :], acc_ref, suffix_mat)

    n_full = (qi * tq) // tk

    def body(i, carry):
        start = pl.multiple_of((n_full - 1 - i) * tk, tk)
        z = _dot_nt(q, k_ref[0, pl.ds(start, tk), :]) * scale
        return _stick_block(z, None, carry, v_ref[0, pl.ds(start, tk), :], acc_ref, suffix_mat)

    lax.fori_loop(0, n_full, body, carry)
    o_ref[...] = acc_ref[...].astype(BF16)


def _stick_prompt(qb_t, kb_t, vb_t, batch, seq):
    tq = _tile(seq, 512, 8)
    tk = _tile(tq, 256, 8)
    nq = seq // tq
    kv_spec = pl.BlockSpec((1, seq, LANES), lambda b, h, i: (h, b, 0))
    return pl.pallas_call(
        functools.partial(_stick_prompt_kernel, tq=tq, tk=tk),
        grid=(batch, HB, nq),
        in_specs=[pl.BlockSpec((1, tq, LANES), lambda b, h, i: (h, b * nq + i, 0)), kv_spec, kv_spec],
        out_specs=pl.BlockSpec((tq, HD_B), lambda b, h, i: (b * nq + i, h)),
        out_shape=jax.ShapeDtypeStruct((batch * seq, SEG_W), BF16),
        scratch_shapes=[pltpu.VMEM((tq, HD_B), F32)],
        compiler_params=_params("parallel", "parallel", "arbitrary"),
        name="stick_prompt",
    )(qb_t, kb_t, vb_t)


def _diff_sample_kernel(q1_ref, q2_ref, kn1_ref, kn2_ref, vn_ref, kc1_ref, kc2_ref, vc_ref, lam_ref, g_ref,
                        o_ref, acc_ref, *, ts, past):
    lam = _lambda(lam_ref)
    q_pos = past + lax.broadcasted_iota(jnp.int32, (ts, 1), 0)
    limit = (q_pos // CHUNK + 1) * CHUNK
    k_pos = past + lax.broadcasted_iota(jnp.int32, (1, ts), 1)
    new_mask = k_pos < limit

    for c in range(2):
        outs = []
        for q_ref, kn_ref, kc_ref in ((q1_ref, kn1_ref, kc1_ref), (q2_ref, kn2_ref, kc2_ref)):
            q = _head_lanes(q_ref[0], c)
            acc_ref[...] = jnp.zeros_like(acc_ref)
            state = (jnp.full((ts, 1), NEG, F32), jnp.zeros((ts, 1), F32))
            s = _dot_nt(q, kc_ref[0].astype(BF16))
            state = _softmax_block(s, state, vc_ref[0, :, c * VD_A:(c + 1) * VD_A].astype(BF16), acc_ref)
            s = jnp.where(new_mask, _dot_nt(q, kn_ref[0]), NEG)
            _, l = _softmax_block(s, state, vn_ref[c], acc_ref)
            outs.append(acc_ref[...] / l)
        o_ref[:, c * VD_A:(c + 1) * VD_A] = _diff_finish(outs[0], outs[1], lam, g_ref[...])


def _diff_sample(qa_t, ka_t, va_t, cache_k, cache_v, lam_vecs, g_head, batch, ts, past):
    half = N_GROUPS // 2
    new_spec = lambda off: pl.BlockSpec((1, ts, LANES), lambda b, p: (p + off, b, 0))
    cache_spec = lambda off: pl.BlockSpec((1, past, LANES), lambda b, p: (b, 0, p + off))
    return pl.pallas_call(
        functools.partial(_diff_sample_kernel, ts=ts, past=past),
        grid=(batch, half),
        in_specs=[new_spec(0), new_spec(half), new_spec(0), new_spec(half),
                  pl.BlockSpec((2, ts, LANES), lambda b, p: (p, b, 0)),
                  cache_spec(0), cache_spec(half),
                  pl.BlockSpec((1, past, 2 * VD_A), lambda b, p: (b, 0, p)),
                  pl.BlockSpec((4, HD_A), lambda b, p: (0, 0)),
                  pl.BlockSpec((1, VD_A), lambda b, p: (0, 0))],
        out_specs=pl.BlockSpec((ts, 2 * VD_A), lambda b, p: (b, p)),
        out_shape=jax.ShapeDtypeStruct((batch * ts, SEG_W), BF16),
        scratch_shapes=[pltpu.VMEM((ts, VD_A), F32)],
        compiler_params=_params("parallel", "parallel"),
        name="diff_sample",
    )(qa_t, qa_t, ka_t, ka_t, va_t, cache_k, cache_k, cache_v, lam_vecs, g_head)


def _stick_sample_kernel(q_ref, kn_ref, vn_ref, kc_ref, vc_ref, o_ref, acc_ref, *, ts, past, tk):
    q = q_ref[0]
    scale = HD_B ** -0.5
    acc_ref[...] = jnp.zeros_like(acc_ref)
    carry = jnp.zeros((ts, 1), F32)
    row = lax.broadcasted_iota(jnp.int32, (ts, 1), 0)
    col = lax.broadcasted_iota(jnp.int32, (1, ts), 1)
    z = _dot_nt(q, kn_ref[0]) * scale
    carry = _stick_block(z, col < row, carry, vn_ref[0], acc_ref, _suffix_matrix(ts))
    suffix_mat = _suffix_matrix(tk)
    for j in range(past // tk - 1, -1, -1):
        k = kc_ref[0, j * tk:(j + 1) * tk, :].astype(BF16)
        v = vc_ref[0, j * tk:(j + 1) * tk, :].astype(BF16)
        carry = _stick_block(_dot_nt(q, k) * scale, None, carry, v, acc_ref, suffix_mat)
    o_ref[...] = acc_ref[...].astype(BF16)


def _stick_sample(qb_t, kb_t, vb_t, cache_k, cache_v, batch, ts, past):
    tk = _tile(past, 256, 8)
    new_spec = pl.BlockSpec((1, ts, LANES), lambda b, h: (h, b, 0))
    cache_spec = pl.BlockSpec((1, past, LANES), lambda b, h: (b, 0, h))
    return pl.pallas_call(
        functools.partial(_stick_sample_kernel, ts=ts, past=past, tk=tk),
        grid=(batch, HB),
        in_specs=[new_spec, new_spec, new_spec, cache_spec, cache_spec],
        out_specs=pl.BlockSpec((ts, HD_B), lambda b, h: (b, h)),
        out_shape=jax.ShapeDtypeStruct((batch * ts, SEG_W), BF16),
        scratch_shapes=[pltpu.VMEM((ts, HD_B), F32)],
        compiler_params=_params("parallel", "parallel"),
        name="stick_sample",
    )(qb_t, kb_t, vb_t, cache_k, cache_v)


def _merge_kernel(h_ref, oa_ref, ob_ref, wga_ref, wgb_ref, wpa_ref, wpb_ref, o_ref):
    h = h_ref[...]
    ga = jax.nn.sigmoid(_dot(h, wga_ref[...]))
    gb = jax.nn.sigmoid(_dot(h, wgb_ref[...]))
    ya = _dot(oa_ref[...], wpa_ref[...])
    yb = _dot(ob_ref[...], wpb_ref[...])
    o_ref[...] = (ga * ya + gb * yb).astype(BF16)


def _merge(h, oa, ob, w_gate, w_proj_a, w_proj_b):
    rows, d = h.shape
    tm = _tile(rows, 512)
    tn = 512
    nn = d // tn
    row_spec = lambda w: pl.BlockSpec((tm, w), lambda i, n: (i, 0))
    return pl.pallas_call(
        _merge_kernel,
        grid=(rows // tm, nn),
        in_specs=[row_spec(d), row_spec(SEG_W), row_spec(SEG_W),
                  pl.BlockSpec((d, tn), lambda i, n: (0, n)),
                  pl.BlockSpec((d, tn), lambda i, n: (0, nn + n)),
                  pl.BlockSpec((SEG_W, tn), lambda i, n: (0, n)),
                  pl.BlockSpec((SEG_W, tn), lambda i, n: (0, n))],
        out_specs=pl.BlockSpec((tm, tn), lambda i, n: (i, n)),
        out_shape=jax.ShapeDtypeStruct((rows, d), BF16),
        compiler_params=_params("parallel", "arbitrary"),
        name="merge",
    )(h, oa, ob, w_gate, w_gate, w_proj_a, w_proj_b)


def _out_kernel(x_ref, m_ref, w_ref, g_post_ref, g_pre_ref, x1_ref, hm_ref):
    x1 = x_ref[...] + _rms(_dot(m_ref[...], w_ref[...]), g_post_ref[...])
    x1_ref[...] = x1
    hm_ref[...] = _rms(x1, g_pre_ref[...]).astype(BF16)


def _out_proj(x, merged, w_out, g_post_mix, g_pre_mlp):
    rows, d = x.shape
    tm = _tile(rows, 256)
    row = pl.BlockSpec((tm, d), lambda i: (i, 0))
    vec = pl.BlockSpec((1, d), lambda i: (0, 0))
    return pl.pallas_call(
        _out_kernel,
        grid=(rows // tm,),
        in_specs=[row, row, pl.BlockSpec((d, d), lambda i: (0, 0)), vec, vec],
        out_specs=[row, row],
        out_shape=[jax.ShapeDtypeStruct((rows, d), F32), jax.ShapeDtypeStruct((rows, d), BF16)],
        compiler_params=_params("parallel"),
        name="out_proj",
    )(x, merged, w_out, g_post_mix, g_pre_mlp)


def _mlp_kernel(x1_ref, hm_ref, wu_ref, wd_ref, g_ref, y_ref, acc_ref):
    f = pl.program_id(1)

    @pl.when(f == 0)
    def _():
        acc_ref[...] = jnp.zeros_like(acc_ref)

    u = jnp.square(jnp.maximum(_dot(hm_ref[...], wu_ref[...]), 0.0))
    acc_ref[...] += _dot(u.astype(BF16), wd_ref[...])

    @pl.when(f == pl.num_programs(1) - 1)
    def _():
        y_ref[...] = x1_ref[...] + _rms(acc_ref[...], g_ref[...])


def _mlp(x1, hm, w_up, w_down, g_post_mlp):
    rows, d = x1.shape
    d_ff = w_up.shape[1]
    tm = _tile(rows, 512)
    tf = 1024
    row = pl.BlockSpec((tm, d), lambda i, f: (i, 0))
    return pl.pallas_call(
        _mlp_kernel,
        grid=(rows // tm, d_ff // tf),
        in_specs=[row, row,
                  pl.BlockSpec((d, tf), lambda i, f: (0, f)),
                  pl.BlockSpec((tf, d), lambda i, f: (f, 0)),
                  pl.BlockSpec((1, d), lambda i, f: (0, 0))],
        out_specs=row,
        out_shape=jax.ShapeDtypeStruct((rows, d), F32),
        scratch_shapes=[pltpu.VMEM((tm, d), F32)],
        compiler_params=_params("parallel", "arbitrary"),
        name="mlp",
    )(x1, hm, w_up, w_down, g_post_mlp)


def _project_all(h, w_segs, cos, sin):
    _, qa_t = _project(h, w_segs[0], "q_rope", cos, sin)
    ka_f, ka_t = _project(h, w_segs[1], "k_rope", cos, sin)
    va_f, va_t = _project(h, w_segs[2], "kv")
    _, qb_t = _project(h, w_segs[3], "q")
    kb_f, kb_t = _project(h, w_segs[4], "kv")
    vb_f, vb_t = _project(h, w_segs[5], "kv")
    return (qa_t, ka_t, va_t, qb_t, kb_t, vb_t), (ka_f, va_f, kb_f, vb_f)


def _tail(x, h, oa, ob, w):
    merged = _merge(h, oa, ob, w["gate"], w["proj_a"], w["proj_b"])
    x1, hm = _out_proj(x, merged, w["out"], w["g_post_mix"], w["g_pre_mlp"])
    return _mlp(x1, hm, w["up"], w["down"], w["g_post_mlp"])


def kernel(x_prompt, x_sample, cache_diff_k, cache_diff_v, cache_sb_k, cache_sb_v, g_pre_mix, w_in, lambda_q1, lambda_k1, lambda_q2, lambda_k2, g_diff_head, w_gate, w_proj_a, w_proj_b, w_out, g_post_mix, g_pre_mlp, w_up, w_down, g_post_mlp):
    bp, tp, d = x_prompt.shape
    bs, ts, _ = x_sample.shape
    depth, _, past = cache_diff_k.shape[:3]
    assert depth == 1 and tp % CHUNK == 0

    w_in_b = w_in[0].astype(BF16)
    w_segs = [w_in_b[:, s * SEG_W:(s + 1) * SEG_W] for s in range(6)]
    w = dict(gate=w_gate[0].astype(BF16), proj_a=w_proj_a[0].astype(BF16), proj_b=w_proj_b[0].astype(BF16),
             out=w_out[0].astype(BF16), up=w_up[0].astype(BF16), down=w_down[0].astype(BF16),
             g_post_mix=g_post_mix, g_pre_mlp=g_pre_mlp, g_post_mlp=g_post_mlp)
    lam_vecs = jnp.concatenate([lambda_q1, lambda_k1, lambda_q2, lambda_k2], axis=0).astype(F32)

    xp = x_prompt.reshape(bp * tp, d)
    hp = _prenorm(xp, g_pre_mix)
    cos_p, sin_p = _rope_tables(jnp.arange(tp, dtype=jnp.int32))
    (qa_t, ka_t, va_t, qb_t, kb_t, vb_t), kv_p = _project_all(hp, w_segs, cos_p, sin_p)
    oa = _diff_prompt(qa_t, ka_t, va_t, lam_vecs, g_diff_head, bp, tp)
    ob = _stick_prompt(qb_t, kb_t, vb_t, bp, tp)
    yp = _tail(xp, hp, oa, ob, w).reshape(bp, tp, d)

    xs = x_sample.reshape(bs * ts, d)
    hs = _prenorm(xs, g_pre_mix)
    pos_s = past + jnp.arange(ts, dtype=jnp.int32)
    cos_s, sin_s = _rope_tables(jnp.tile(pos_s, bs))
    (qa_t, ka_t, va_t, qb_t, kb_t, vb_t), kv_s = _project_all(hs, w_segs, cos_s, sin_s)
    oa = _diff_sample(qa_t, ka_t, va_t, cache_diff_k[0].reshape(bs, past, SEG_W),
                      cache_diff_v[0].reshape(bs, past, SEG_W), lam_vecs, g_diff_head, bs, ts, past)
    ob = _stick_sample(qb_t, kb_t, vb_t, cache_sb_k[0].reshape(bs, past, SEG_W),
                       cache_sb_v[0].reshape(bs, past, SEG_W), bs, ts, past)
    ys = _tail(xs, hs, oa, ob, w).reshape(bs, ts, d)

    def caches(kv, b, t):
        ka_f, va_f, kb_f, vb_f = kv
        return (ka_f.reshape(1, b, t, 2 * HA, HD_A), va_f.reshape(1, b, t, HA, VD_A),
                kb_f.reshape(1, b, t, HB, HD_B), vb_f.reshape(1, b, t, HB, HD_B))

    return (yp, ys) + caches(kv_p, bp, tp) + caches(kv_s, bs, ts)
```

```python
import functools
import math

import jax
import jax.numpy as jnp
from jax import lax
from jax.experimental import pallas as pl
from jax.experimental.pallas import tpu as pltpu

F32 = jnp.float32
BF16 = jnp.bfloat16

CHUNK = 64
HA = 8
HD_A = 64
VD_A = 128
HB = 8
HD_B = 128
SEG_W = 1024
LANES = 128
N_GROUPS = SEG_W // LANES
ROPE_THETA = 10000.0
EPS = 1e-6
NEG = -1e30
LAM_INIT = 0.8 - 0.6 * math.exp(-0.3 * 0)
UNDERFLOW_LOG = -104.0
VMEM_LIMIT = 56 * 1024 * 1024


def _tile(n, pref, mult=8):
    if n <= pref:
        return n
    for t in range(pref, 0, -1):
        if n % t == 0 and t % mult == 0:
            return t
    raise ValueError(f"no tile for {n}")


def _params(*sem):
    return pltpu.CompilerParams(dimension_semantics=sem, vmem_limit_bytes=VMEM_LIMIT)


def _dot(a, b):
    return jnp.dot(a, b, preferred_element_type=F32)


def _dot_nt(a, b):
    return lax.dot_general(a, b, (((1,), (1,)), ((), ())), preferred_element_type=F32)


def _rms(x, g):
    return x * lax.rsqrt(jnp.mean(x * x, axis=-1, keepdims=True) + EPS) * g


def _prenorm_kernel(x_ref, g_ref, h_ref):
    h_ref[...] = _rms(x_ref[...], g_ref[...]).astype(BF16)


def _prenorm(x, g):
    rows, d = x.shape
    tm = _tile(rows, 512)
    return pl.pallas_call(
        _prenorm_kernel,
        grid=(rows // tm,),
        in_specs=[pl.BlockSpec((tm, d), lambda i: (i, 0)), pl.BlockSpec((1, d), lambda i: (0, 0))],
        out_specs=pl.BlockSpec((tm, d), lambda i: (i, 0)),
        out_shape=jax.ShapeDtypeStruct((rows, d), BF16),
        compiler_params=_params("parallel"),
        name="prenorm",
    )(x, g)


def _store_groups(t_ref, z):
    for c in range(N_GROUPS):
        t_ref[c] = z[:, c * LANES:(c + 1) * LANES].astype(BF16)


def _rope_groups(z, cos, sin):
    lane = lax.broadcasted_iota(jnp.int32, (1, LANES), 1)
    first_half = (lane % HD_A) < (HD_A // 2)
    out = []
    for c in range(N_GROUPS):
        zc = z[:, c * LANES:(c + 1) * LANES]
        partner = jnp.where(first_half, pltpu.roll(zc, LANES - HD_A // 2, 1), pltpu.roll(zc, HD_A // 2, 1))
        out.append(zc * cos + partner * sin)
    return out


def _proj_kv_kernel(h_ref, w_ref, f_ref, t_ref):
    z = _dot(h_ref[...], w_ref[...])
    f_ref[...] = z
    _store_groups(t_ref, z)


def _proj_v_blocks_kernel(h_ref, w_ref, f_ref, t_ref, *, tk):
    z = _dot(h_ref[...], w_ref[...])
    f_ref[...] = z
    for c in range(N_GROUPS):
        zt = z[:, c * LANES:(c + 1) * LANES].T
        for j in range(zt.shape[1] // tk):
            t_ref[c, j] = zt[:, j * tk:(j + 1) * tk].astype(BF16)


def _proj_q_kernel(h_ref, w_ref, t_ref):
    _store_groups(t_ref, _dot(h_ref[...], w_ref[...]))


def _proj_k_rope_kernel(h_ref, w_ref, cos_ref, sin_ref, f_ref, t_ref):
    z = _dot(h_ref[...], w_ref[...])
    for c, zc in enumerate(_rope_groups(z, cos_ref[...], sin_ref[...])):
        f_ref[:, c * LANES:(c + 1) * LANES] = zc
        t_ref[c] = zc.astype(BF16)


def _proj_q_rope_kernel(h_ref, w_ref, cos_ref, sin_ref, t_ref):
    z = _dot(h_ref[...], w_ref[...])
    for c, zc in enumerate(_rope_groups(z, cos_ref[...], sin_ref[...])):
        t_ref[c] = (zc * (HD_A ** -0.5)).astype(BF16)


def _project(h, w, mode, cos=None, sin=None, tk=None):
    rows, d = h.shape
    tm = _tile(rows, 512)
    n_tab = None if cos is None else cos.shape[0] // tm
    in_specs = [pl.BlockSpec((tm, d), lambda i: (i, 0)), pl.BlockSpec((d, SEG_W), lambda i: (0, 0))]
    args = [h, w]
    if cos is not None:
        tab = pl.BlockSpec((tm, LANES), lambda i: (i % n_tab, 0))
        in_specs += [tab, tab]
        args += [cos, sin]
    f_spec = pl.BlockSpec((tm, SEG_W), lambda i: (i, 0))
    t_spec = pl.BlockSpec((N_GROUPS, tm, LANES), lambda i: (0, i, 0))
    f_shape = jax.ShapeDtypeStruct((rows, SEG_W), F32)
    t_shape = jax.ShapeDtypeStruct((N_GROUPS, rows, LANES), BF16)
    with_f32 = mode in ("kv", "k_rope", "v_blocks")
    body = {"kv": _proj_kv_kernel, "q": _proj_q_kernel, "k_rope": _proj_k_rope_kernel,
            "q_rope": _proj_q_rope_kernel, "v_blocks": functools.partial(_proj_v_blocks_kernel, tk=tk)}[mode]
    if mode == "v_blocks":
        t_spec = pl.BlockSpec((N_GROUPS, tm // tk, LANES, tk), lambda i: (0, i, 0, 0))
        t_shape = jax.ShapeDtypeStruct((N_GROUPS, rows // tk, LANES, tk), BF16)
    out = pl.pallas_call(
        body,
        grid=(rows // tm,),
        in_specs=in_specs,
        out_specs=[f_spec, t_spec] if with_f32 else t_spec,
        out_shape=[f_shape, t_shape] if with_f32 else t_shape,
        compiler_params=_params("parallel"),
        name="proj_" + mode,
    )(*args)
    return (out[0], out[1]) if with_f32 else (None, out)


def _rope_tables(pos):
    half = HD_A // 2
    inv = ROPE_THETA ** (-jnp.arange(half, dtype=F32) / half)
    ang = pos.astype(F32)[:, None] * inv[None, :]
    cos, sin = jnp.cos(ang), jnp.sin(ang)
    reps = LANES // HD_A
    return (jnp.tile(jnp.concatenate([cos, cos], axis=-1), (1, reps)),
            jnp.tile(jnp.concatenate([-sin, sin], axis=-1), (1, reps)))


def _head_lanes(q, c):
    lane = lax.broadcasted_iota(jnp.int32, (1, LANES), 1)
    return jnp.where((lane >= c * HD_A) & (lane < (c + 1) * HD_A), q, jnp.zeros_like(q))


def _softmax_block(s, state, v, acc_ref):
    m, l = state
    m_new = jnp.maximum(m, jnp.max(s, axis=-1, keepdims=True))
    alpha = jnp.exp(m - m_new)
    p = jnp.exp(s - m_new)
    l_new = alpha * l + jnp.sum(p, axis=-1, keepdims=True)
    acc_ref[...] = alpha * acc_ref[...] + _dot(p.astype(BF16), v)
    return m_new, l_new


def _suffix_matrix(n):
    j = lax.broadcasted_iota(jnp.int32, (n, n), 0)
    s = lax.broadcasted_iota(jnp.int32, (n, n), 1)
    return jnp.where(j >= s, 1.0, 0.0).astype(BF16)


def _stick_block(z, mask, carry, v, acc_ref, suffix_mat):
    log_keep = jnp.minimum(-z, 0.0) - jnp.log(1.0 + jnp.exp(-jnp.abs(z)))
    if mask is not None:
        log_keep = jnp.where(mask, log_keep, 0.0)
    hi = log_keep.astype(BF16)
    lo = (log_keep - hi.astype(F32)).astype(BF16)
    suffix = _dot(hi, suffix_mat) + _dot(lo, suffix_mat)
    a = jnp.exp(z + suffix + carry)
    if mask is not None:
        a = jnp.where(mask, a, 0.0)
    acc_ref[...] += _dot(a.astype(BF16), v)
    return carry + suffix[:, 0:1]


def _lambda(lam_ref):
    lam = lam_ref[...]
    d1 = jnp.sum(lam[0:1] * lam[1:2], axis=-1, keepdims=True)
    d2 = jnp.sum(lam[2:3] * lam[3:4], axis=-1, keepdims=True)
    return jnp.exp(d1) - jnp.exp(d2) + LAM_INIT


def _diff_finish(o1, o2, lam, g_head):
    o = o1 - lam * o2
    o = o * lax.rsqrt(jnp.mean(o * o, axis=-1, keepdims=True) + EPS)
    return (o * g_head * (1.0 - LAM_INIT)).astype(BF16)


def _diff_prompt_kernel(q1_ref, q2_ref, k1_ref, k2_ref, vt_ref, lam_ref, g_ref, o_ref, acc_ref, s_ref, p_ref, *, tq, tk):
    qi = pl.program_id(2)
    lam = _lambda(lam_ref)
    n_full = (qi * tq) // tk
    q_pos = qi * tq + lax.broadcasted_iota(jnp.int32, (1, tq), 1)
    limit = (q_pos // CHUNK + 1) * CHUNK
    k_pos = n_full * tk + lax.broadcasted_iota(jnp.int32, (tk, 1), 0)
    edge_mask = k_pos < limit

    streams = [(c, _head_lanes(q_ref[0], c), k_ref)
               for c in range(2) for q_ref, k_ref in ((q1_ref, k1_ref), (q2_ref, k2_ref))]
    acc_ref[...] = jnp.zeros_like(acc_ref)

    def score_block(j):
        start = pl.multiple_of(j * tk, tk)
        return [_dot_nt(k_ref[0, pl.ds(start, tk), :], q) for _, q, k_ref in streams]

    def softmax_block(state, mask):
        new_state, alphas = [], []
        for idx in range(len(streams)):
            s = s_ref[idx]
            if mask is not None:
                s = jnp.where(mask, s, NEG)
            m, l = state[idx]
            m_new = jnp.maximum(m, jnp.max(s, axis=0, keepdims=True))
            alpha = jnp.exp(m - m_new)
            p = jnp.exp(s - m_new)
            new_state.append((m_new, alpha * l + jnp.sum(p, axis=0, keepdims=True)))
            alphas.append(alpha)
            p_ref[idx] = p.astype(BF16)
        return tuple(new_state), tuple(alphas)

    def value_block(j, alphas):
        for idx, (c, _, _) in enumerate(streams):
            acc_ref[idx] = alphas[idx] * acc_ref[idx] + _dot(vt_ref[c, j], p_ref[idx])

    def step(j, carry):
        state, alphas = carry
        next_scores = score_block(j + 1)
        value_block(jnp.maximum(j - 1, 0), alphas)
        carry = softmax_block(state, None)
        for idx, s in enumerate(next_scores):
            s_ref[idx] = s
        return carry

    for idx, s in enumerate(score_block(0)):
        s_ref[idx] = s
    p_ref[...] = jnp.zeros_like(p_ref)
    state = tuple((jnp.full((1, tq), NEG, F32), jnp.zeros((1, tq), F32)) for _ in streams)
    alphas = tuple(jnp.ones((1, tq), F32) for _ in streams)
    state, alphas = lax.fori_loop(0, n_full, step, (state, alphas))
    value_block(jnp.maximum(n_full - 1, 0), alphas)
    state, alphas = softmax_block(state, edge_mask)
    value_block(n_full, alphas)
    for c in range(2):
        o = acc_ref[2 * c] / state[2 * c][1] - lam * (acc_ref[2 * c + 1] / state[2 * c + 1][1])
        o = o * lax.rsqrt(jnp.mean(o * o, axis=0, keepdims=True) + EPS)
        o = o * g_ref[...] * (1.0 - LAM_INIT)
        o_ref[:, c * VD_A:(c + 1) * VD_A] = o.T.astype(BF16)


def _diff_prompt(qa_t, ka_t, va_tt, lam_vecs, g_head, batch, seq, tk):
    tq = _tile(tk, 256, CHUNK)
    nq = seq // tq
    half = N_GROUPS // 2
    q_spec = lambda off: pl.BlockSpec((1, tq, LANES), lambda b, p, i: (p + off, b * nq + i, 0))
    k_spec = lambda off: pl.BlockSpec((1, seq, LANES), lambda b, p, i: (p + off, b, 0))
    return pl.pallas_call(
        functools.partial(_diff_prompt_kernel, tq=tq, tk=tk),
        grid=(batch, half, nq),
        in_specs=[q_spec(0), q_spec(half), k_spec(0), k_spec(half),
                  pl.BlockSpec((2, seq // tk, VD_A, tk), lambda b, p, i: (p, b, 0, 0)),
                  pl.BlockSpec((4, HD_A), lambda b, p, i: (0, 0)),
                  pl.BlockSpec((VD_A, 1), lambda b, p, i: (0, 0))],
        out_specs=pl.BlockSpec((tq, 2 * VD_A), lambda b, p, i: (b * nq + i, p)),
        out_shape=jax.ShapeDtypeStruct((batch * seq, SEG_W), BF16),
        scratch_shapes=[pltpu.VMEM((4, VD_A, tq), F32), pltpu.VMEM((4, tk, tq), F32),
                        pltpu.VMEM((4, tk, tq), BF16)],
        compiler_params=_params("parallel", "parallel", "arbitrary"),
        name="diff_prompt",
    )(qa_t, qa_t, ka_t, ka_t, va_tt, lam_vecs, g_head.reshape(VD_A, 1))


def _stick_prompt_kernel(q_ref, k_ref, v_ref, o_ref, acc_ref, *, tq, tk):
    qi = pl.program_id(2)
    q = q_ref[0]
    suffix_mat = _suffix_matrix(tk)
    scale = HD_B ** -0.5
    acc_ref[...] = jnp.zeros_like(acc_ref)
    carry = jnp.zeros((tq, 1), F32)
    q_pos = qi * tq + lax.broadcasted_iota(jnp.int32, (tq, 1), 0)

    n_edge = tq // tk
    for e in range(n_edge - 1, -1, -1):
        start = pl.multiple_of(qi * tq + e * tk, tk)
        k_pos = start + lax.broadcasted_iota(jnp.int32, (1, tk), 1)
        z = _dot_nt(q, k_ref[0, pl.ds(start, tk), :]) * scale
        carry = _stick_block(z, k_pos < q_pos, carry, v_ref[0, pl.ds(start, tk), :], acc_ref, suffix_mat)

    n_full = (qi * tq) // tk

    def cond(st):
        i, carry = st
        return jnp.logical_and(i < n_full, jnp.max(carry) > UNDERFLOW_LOG)

    def body(st):
        i, carry = st
        start = pl.multiple_of((n_full - 1 - i) * tk, tk)
        z = _dot_nt(q, k_ref[0, pl.ds(start, tk), :]) * scale
        return i + 1, _stick_block(z, None, carry, v_ref[0, pl.ds(start, tk), :], acc_ref, suffix_mat)

    lax.while_loop(cond, body, (jnp.int32(0), carry))
    o_ref[...] = acc_ref[...].astype(BF16)


def _stick_prompt(qb_t, kb_t, vb_t, batch, seq):
    tq = _tile(seq, 512, 8)
    tk = _tile(tq, 256, 8)
    nq = seq // tq
    kv_spec = pl.BlockSpec((1, seq, LANES), lambda b, h, i: (h, b, 0))
    return pl.pallas_call(
        functools.partial(_stick_prompt_kernel, tq=tq, tk=tk),
        grid=(batch, HB, nq),
        in_specs=[pl.BlockSpec((1, tq, LANES), lambda b, h, i: (h, b * nq + i, 0)), kv_spec, kv_spec],
        out_specs=pl.BlockSpec((tq, HD_B), lambda b, h, i: (b * nq + i, h)),
        out_shape=jax.ShapeDtypeStruct((batch * seq, SEG_W), BF16),
        scratch_shapes=[pltpu.VMEM((tq, HD_B), F32)],
        compiler_params=_params("parallel", "parallel", "arbitrary"),
        name="stick_prompt",
    )(qb_t, kb_t, vb_t)


def _diff_sample_kernel(q1_ref, q2_ref, kn1_ref, kn2_ref, vn_ref, kc1_ref, kc2_ref, vc_ref, lam_ref, g_ref,
                        o_ref, acc_ref, *, ts, past):
    lam = _lambda(lam_ref)
    q_pos = past + lax.broadcasted_iota(jnp.int32, (ts, 1), 0)
    limit = (q_pos // CHUNK + 1) * CHUNK
    k_pos = past + lax.broadcasted_iota(jnp.int32, (1, ts), 1)
    new_mask = k_pos < limit

    for c in range(2):
        outs = []
        for q_ref, kn_ref, kc_ref in ((q1_ref, kn1_ref, kc1_ref), (q2_ref, kn2_ref, kc2_ref)):
            q = _head_lanes(q_ref[0], c)
            acc_ref[...] = jnp.zeros_like(acc_ref)
            state = (jnp.full((ts, 1), NEG, F32), jnp.zeros((ts, 1), F32))
            s = _dot_nt(q, kc_ref[0].astype(BF16))
            state = _softmax_block(s, state, vc_ref[0, :, c * VD_A:(c + 1) * VD_A].astype(BF16), acc_ref)
            s = jnp.where(new_mask, _dot_nt(q, kn_ref[0]), NEG)
            _, l = _softmax_block(s, state, vn_ref[c], acc_ref)
            outs.append(acc_ref[...] / l)
        o_ref[:, c * VD_A:(c + 1) * VD_A] = _diff_finish(outs[0], outs[1], lam, g_ref[...])


def _diff_sample(qa_t, ka_t, va_t, cache_k, cache_v, lam_vecs, g_head, batch, ts, past):
    half = N_GROUPS // 2
    new_spec = lambda off: pl.BlockSpec((1, ts, LANES), lambda b, p: (p + off, b, 0))
    cache_spec = lambda off: pl.BlockSpec((1, past, LANES), lambda b, p: (b, 0, p + off))
    return pl.pallas_call(
        functools.partial(_diff_sample_kernel, ts=ts, past=past),
        grid=(batch, half),
        in_specs=[new_spec(0), new_spec(half), new_spec(0), new_spec(half),
                  pl.BlockSpec((2, ts, LANES), lambda b, p: (p, b, 0)),
                  cache_spec(0), cache_spec(half),
                  pl.BlockSpec((1, past, 2 * VD_A), lambda b, p: (b, 0, p)),
                  pl.BlockSpec((4, HD_A), lambda b, p: (0, 0)),
                  pl.BlockSpec((1, VD_A), lambda b, p: (0, 0))],
        out_specs=pl.BlockSpec((ts, 2 * VD_A), lambda b, p: (b, p)),
        out_shape=jax.ShapeDtypeStruct((batch * ts, SEG_W), BF16),
        scratch_shapes=[pltpu.VMEM((ts, VD_A), F32)],
        compiler_params=_params("parallel", "parallel"),
        name="diff_sample",
    )(qa_t, qa_t, ka_t, ka_t, va_t, cache_k, cache_k, cache_v, lam_vecs, g_head)


def _stick_sample_kernel(q_ref, kn_ref, vn_ref, kc_ref, vc_ref, o_ref, acc_ref, *, ts, past, tk):
    q = q_ref[0]
    scale = HD_B ** -0.5
    acc_ref[...] = jnp.zeros_like(acc_ref)
    carry = jnp.zeros((ts, 1), F32)
    row = lax.broadcasted_iota(jnp.int32, (ts, 1), 0)
    col = lax.broadcasted_iota(jnp.int32, (1, ts), 1)
    z = _dot_nt(q, kn_ref[0]) * scale
    carry = _stick_block(z, col < row, carry, vn_ref[0], acc_ref, _suffix_matrix(ts))
    suffix_mat = _suffix_matrix(tk)
    n_full = past // tk

    def cond(st):
        i, carry = st
        return jnp.logical_and(i < n_full, jnp.max(carry) > UNDERFLOW_LOG)

    def body(st):
        i, carry = st
        start = pl.multiple_of((n_full - 1 - i) * tk, tk)
        k = kc_ref[0, pl.ds(start, tk), :].astype(BF16)
        v = vc_ref[0, pl.ds(start, tk), :].astype(BF16)
        return i + 1, _stick_block(_dot_nt(q, k) * scale, None, carry, v, acc_ref, suffix_mat)

    lax.while_loop(cond, body, (jnp.int32(0), carry))
    o_ref[...] = acc_ref[...].astype(BF16)


def _stick_sample(qb_t, kb_t, vb_t, cache_k, cache_v, batch, ts, past):
    tk = _tile(past, 256, 8)
    new_spec = pl.BlockSpec((1, ts, LANES), lambda b, h: (h, b, 0))
    cache_spec = pl.BlockSpec((1, past, LANES), lambda b, h: (b, 0, h))
    return pl.pallas_call(
        functools.partial(_stick_sample_kernel, ts=ts, past=past, tk=tk),
        grid=(batch, HB),
        in_specs=[new_spec, new_spec, new_spec, cache_spec, cache_spec],
        out_specs=pl.BlockSpec((ts, HD_B), lambda b, h: (b, h)),
        out_shape=jax.ShapeDtypeStruct((batch * ts, SEG_W), BF16),
        scratch_shapes=[pltpu.VMEM((ts, HD_B), F32)],
        compiler_params=_params("parallel", "parallel"),
        name="stick_sample",
    )(qb_t, kb_t, vb_t, cache_k, cache_v)


def _merge_kernel(h_ref, oa_ref, ob_ref, wga_ref, wgb_ref, wpa_ref, wpb_ref, o_ref):
    h = h_ref[...]
    ga = jax.nn.sigmoid(_dot(h, wga_ref[...]))
    gb = jax.nn.sigmoid(_dot(h, wgb_ref[...]))
    ya = _dot(oa_ref[...], wpa_ref[...])
    yb = _dot(ob_ref[...], wpb_ref[...])
    o_ref[...] = (ga * ya + gb * yb).astype(BF16)


def _merge(h, oa, ob, w_gate, w_proj_a, w_proj_b):
    rows, d = h.shape
    tm = _tile(rows, 512)
    tn = 512
    nn = d // tn
    row_spec = lambda w: pl.BlockSpec((tm, w), lambda i, n: (i, 0))
    return pl.pallas_call(
        _merge_kernel,
        grid=(rows // tm, nn),
        in_specs=[row_spec(d), row_spec(SEG_W), row_spec(SEG_W),
                  pl.BlockSpec((d, tn), lambda i, n: (0, n)),
                  pl.BlockSpec((d, tn), lambda i, n: (0, nn + n)),
                  pl.BlockSpec((SEG_W, tn), lambda i, n: (0, n)),
                  pl.BlockSpec((SEG_W, tn), lambda i, n: (0, n))],
        out_specs=pl.BlockSpec((tm, tn), lambda i, n: (i, n)),
        out_shape=jax.ShapeDtypeStruct((rows, d), BF16),
        compiler_params=_params("parallel", "arbitrary"),
        name="merge",
    )(h, oa, ob, w_gate, w_gate, w_proj_a, w_proj_b)


def _out_kernel(x_ref, m_ref, w_ref, g_post_ref, g_pre_ref, x1_ref, hm_ref):
    x1 = x_ref[...] + _rms(_dot(m_ref[...], w_ref[...]), g_post_ref[...])
    x1_ref[...] = x1
    hm_ref[...] = _rms(x1, g_pre_ref[...]).astype(BF16)


def _out_proj(x, merged, w_out, g_post_mix, g_pre_mlp):
    rows, d = x.shape
    tm = _tile(rows, 256)
    row = pl.BlockSpec((tm, d), lambda i: (i, 0))
    vec = pl.BlockSpec((1, d), lambda i: (0, 0))
    return pl.pallas_call(
        _out_kernel,
        grid=(rows // tm,),
        in_specs=[row, row, pl.BlockSpec((d, d), lambda i: (0, 0)), vec, vec],
        out_specs=[row, row],
        out_shape=[jax.ShapeDtypeStruct((rows, d), F32), jax.ShapeDtypeStruct((rows, d), BF16)],
        compiler_params=_params("parallel"),
        name="out_proj",
    )(x, merged, w_out, g_post_mix, g_pre_mlp)


def _mlp_kernel(x1_ref, hm_ref, wu_ref, wd_ref, g_ref, y_ref, acc_ref):
    f = pl.program_id(1)

    @pl.when(f == 0)
    def _():
        acc_ref[...] = jnp.zeros_like(acc_ref)

    u = jnp.square(jnp.maximum(_dot(hm_ref[...], wu_ref[...]), 0.0))
    acc_ref[...] += _dot(u.astype(BF16), wd_ref[...])

    @pl.when(f == pl.num_programs(1) - 1)
    def _():
        y_ref[...] = x1_ref[...] + _rms(acc_ref[...], g_ref[...])


def _mlp(x1, hm, w_up, w_down, g_post_mlp):
    rows, d = x1.shape
    d_ff = w_up.shape[1]
    tm = _tile(rows, 512)
    tf = 1024
    row = pl.BlockSpec((tm, d), lambda i, f: (i, 0))
    return pl.pallas_call(
        _mlp_kernel,
        grid=(rows // tm, d_ff // tf),
        in_specs=[row, row,
                  pl.BlockSpec((d, tf), lambda i, f: (0, f)),
                  pl.BlockSpec((tf, d), lambda i, f: (f, 0)),
                  pl.BlockSpec((1, d), lambda i, f: (0, 0))],
        out_specs=row,
        out_shape=jax.ShapeDtypeStruct((rows, d), F32),
        scratch_shapes=[pltpu.VMEM((tm, d), F32)],
        compiler_params=_params("parallel", "arbitrary"),
        name="mlp",
    )(x1, hm, w_up, w_down, g_post_mlp)


def _project_all(h, w_segs, cos, sin, v_block=None):
    _, qa_t = _project(h, w_segs[0], "q_rope", cos, sin)
    ka_f, ka_t = _project(h, w_segs[1], "k_rope", cos, sin)
    if v_block is None:
        va_f, va_t = _project(h, w_segs[2], "kv")
    else:
        va_f, va_t = _project(h, w_segs[2], "v_blocks", tk=v_block)
    _, qb_t = _project(h, w_segs[3], "q")
    kb_f, kb_t = _project(h, w_segs[4], "kv")
    vb_f, vb_t = _project(h, w_segs[5], "kv")
    return (qa_t, ka_t, va_t, qb_t, kb_t, vb_t), (ka_f, va_f, kb_f, vb_f)


def _tail(x, h, oa, ob, w):
    merged = _merge(h, oa, ob, w["gate"], w["proj_a"], w["proj_b"])
    x1, hm = _out_proj(x, merged, w["out"], w["g_post_mix"], w["g_pre_mlp"])
    return _mlp(x1, hm, w["up"], w["down"], w["g_post_mlp"])


def kernel(x_prompt, x_sample, cache_diff_k, cache_diff_v, cache_sb_k, cache_sb_v, g_pre_mix, w_in, lambda_q1, lambda_k1, lambda_q2, lambda_k2, g_diff_head, w_gate, w_proj_a, w_proj_b, w_out, g_post_mix, g_pre_mlp, w_up, w_down, g_post_mlp):
    bp, tp, d = x_prompt.shape
    bs, ts, _ = x_sample.shape
    depth, _, past = cache_diff_k.shape[:3]
    assert depth == 1 and tp % CHUNK == 0

    w_in_b = w_in[0].astype(BF16)
    w_segs = [w_in_b[:, s * SEG_W:(s + 1) * SEG_W] for s in range(6)]
    w = dict(gate=w_gate[0].astype(BF16), proj_a=w_proj_a[0].astype(BF16), proj_b=w_proj_b[0].astype(BF16),
             out=w_out[0].astype(BF16), up=w_up[0].astype(BF16), down=w_down[0].astype(BF16),
             g_post_mix=g_post_mix, g_pre_mlp=g_pre_mlp, g_post_mlp=g_post_mlp)
    lam_vecs = jnp.concatenate([lambda_q1, lambda_k1, lambda_q2, lambda_k2], axis=0).astype(F32)

    xp = x_prompt.reshape(bp * tp, d)
    hp = _prenorm(xp, g_pre_mix)
    cos_p, sin_p = _rope_tables(jnp.arange(tp, dtype=jnp.int32))
    tk_diff = _tile(tp, 256, CHUNK)
    (qa_t, ka_t, va_tt, qb_t, kb_t, vb_t), kv_p = _project_all(hp, w_segs, cos_p, sin_p, v_block=tk_diff)
    oa = _diff_prompt(qa_t, ka_t, va_tt, lam_vecs, g_diff_head, bp, tp, tk_diff)
    ob = _stick_prompt(qb_t, kb_t, vb_t, bp, tp)
    yp = _tail(xp, hp, oa, ob, w).reshape(bp, tp, d)

    xs = x_sample.reshape(bs * ts, d)
    hs = _prenorm(xs, g_pre_mix)
    pos_s = past + jnp.arange(ts, dtype=jnp.int32)
    cos_s, sin_s = _rope_tables(jnp.tile(pos_s, bs))
    (qa_t, ka_t, va_t, qb_t, kb_t, vb_t), kv_s = _project_all(hs, w_segs, cos_s, sin_s)
    oa = _diff_sample(qa_t, ka_t, va_t, cache_diff_k[0].reshape(bs, past, SEG_W),
                      cache_diff_v[0].reshape(bs, past, SEG_W), lam_vecs, g_diff_head, bs, ts, past)
    ob = _stick_sample(qb_t, kb_t, vb_t, cache_sb_k[0].reshape(bs, past, SEG_W),
                       cache_sb_v[0].reshape(bs, past, SEG_W), bs, ts, past)
    ys = _tail(xs, hs, oa, ob, w).reshape(bs, ts, d)

    def caches(kv, b, t):
        ka_f, va_f, kb_f, vb_f = kv
        return (ka_f.reshape(1, b, t, 2 * HA, HD_A), va_f.reshape(1, b, t, HA, VD_A),
                kb_f.reshape(1, b, t, HB, HD_B), vb_f.reshape(1, b, t, HB, HD_B))

    return (yp, ys) + caches(kv_p, bp, tp) + caches(kv_s, bs, ts)
```

```python
import functools
import math

import jax
import jax.numpy as jnp
from jax import lax
from jax.experimental import pallas as pl
from jax.experimental.pallas import tpu as pltpu

F32 = jnp.float32
BF16 = jnp.bfloat16

CHUNK = 64
HA = 8
HD_A = 64
VD_A = 128
HB = 8
HD_B = 128
SEG_W = 1024
LANES = 128
N_GROUPS = SEG_W // LANES
ROPE_THETA = 10000.0
EPS = 1e-6
NEG = -1e30
LAM_INIT = 0.8 - 0.6 * math.exp(-0.3 * 0)
UNDERFLOW_LOG = -104.0
DENOM_MIN = 1e-20
ONES_ROWS = 16
VMEM_LIMIT = 56 * 1024 * 1024


def _tile(n, pref, mult=8):
    if n <= pref:
        return n
    for t in range(pref, 0, -1):
        if n % t == 0 and t % mult == 0:
            return t
    raise ValueError(f"no tile for {n}")


def _params(*sem):
    return pltpu.CompilerParams(dimension_semantics=sem, vmem_limit_bytes=VMEM_LIMIT)


def _dot(a, b):
    return jnp.dot(a, b, preferred_element_type=F32)


def _dot_nt(a, b):
    return lax.dot_general(a, b, (((1,), (1,)), ((), ())), preferred_element_type=F32)


def _rms(x, g):
    return x * lax.rsqrt(jnp.mean(x * x, axis=-1, keepdims=True) + EPS) * g


def _prenorm_kernel(x_ref, g_ref, h_ref):
    h_ref[...] = _rms(x_ref[...], g_ref[...]).astype(BF16)


def _prenorm(x, g):
    rows, d = x.shape
    tm = _tile(rows, 512)
    return pl.pallas_call(
        _prenorm_kernel,
        grid=(rows // tm,),
        in_specs=[pl.BlockSpec((tm, d), lambda i: (i, 0)), pl.BlockSpec((1, d), lambda i: (0, 0))],
        out_specs=pl.BlockSpec((tm, d), lambda i: (i, 0)),
        out_shape=jax.ShapeDtypeStruct((rows, d), BF16),
        compiler_params=_params("parallel"),
        name="prenorm",
    )(x, g)


def _store_groups(t_ref, z):
    for c in range(N_GROUPS):
        t_ref[c] = z[:, c * LANES:(c + 1) * LANES].astype(BF16)


def _rope_groups(z, cos, sin):
    lane = lax.broadcasted_iota(jnp.int32, (1, LANES), 1)
    first_half = (lane % HD_A) < (HD_A // 2)
    out = []
    for c in range(N_GROUPS):
        zc = z[:, c * LANES:(c + 1) * LANES]
        partner = jnp.where(first_half, pltpu.roll(zc, LANES - HD_A // 2, 1), pltpu.roll(zc, HD_A // 2, 1))
        out.append(zc * cos + partner * sin)
    return out


def _proj_kv_kernel(h_ref, w_ref, f_ref, t_ref):
    z = _dot(h_ref[...], w_ref[...])
    f_ref[...] = z
    _store_groups(t_ref, z)


def _proj_v_blocks_kernel(h_ref, w_ref, f_ref, t_ref, *, tk):
    z = _dot(h_ref[...], w_ref[...])
    f_ref[...] = z
    for c in range(N_GROUPS):
        zt = z[:, c * LANES:(c + 1) * LANES].T
        for j in range(zt.shape[1] // tk):
            t_ref[c, j, :LANES, :] = zt[:, j * tk:(j + 1) * tk].astype(BF16)
    row = lax.broadcasted_iota(jnp.int32, (ONES_ROWS, tk), 0)
    ones_rows = jnp.where(row == 0, 1.0, 0.0).astype(BF16)
    for c in range(N_GROUPS):
        for j in range(t_ref.shape[1]):
            t_ref[c, j, LANES:, :] = ones_rows


def _proj_q_kernel(h_ref, w_ref, t_ref):
    _store_groups(t_ref, _dot(h_ref[...], w_ref[...]))


def _proj_k_rope_kernel(h_ref, w_ref, cos_ref, sin_ref, f_ref, t_ref):
    z = _dot(h_ref[...], w_ref[...])
    for c, zc in enumerate(_rope_groups(z, cos_ref[...], sin_ref[...])):
        f_ref[:, c * LANES:(c + 1) * LANES] = zc
        t_ref[c] = zc.astype(BF16)


def _proj_q_rope_kernel(h_ref, w_ref, cos_ref, sin_ref, t_ref):
    z = _dot(h_ref[...], w_ref[...])
    for c, zc in enumerate(_rope_groups(z, cos_ref[...], sin_ref[...])):
        t_ref[c] = (zc * (HD_A ** -0.5)).astype(BF16)


def _project(h, w, mode, cos=None, sin=None, tk=None):
    rows, d = h.shape
    tm = _tile(rows, 512)
    n_tab = None if cos is None else cos.shape[0] // tm
    in_specs = [pl.BlockSpec((tm, d), lambda i: (i, 0)), pl.BlockSpec((d, SEG_W), lambda i: (0, 0))]
    args = [h, w]
    if cos is not None:
        tab = pl.BlockSpec((tm, LANES), lambda i: (i % n_tab, 0))
        in_specs += [tab, tab]
        args += [cos, sin]
    f_spec = pl.BlockSpec((tm, SEG_W), lambda i: (i, 0))
    t_spec = pl.BlockSpec((N_GROUPS, tm, LANES), lambda i: (0, i, 0))
    f_shape = jax.ShapeDtypeStruct((rows, SEG_W), F32)
    t_shape = jax.ShapeDtypeStruct((N_GROUPS, rows, LANES), BF16)
    with_f32 = mode in ("kv", "k_rope", "v_blocks")
    body = {"kv": _proj_kv_kernel, "q": _proj_q_kernel, "k_rope": _proj_k_rope_kernel,
            "q_rope": _proj_q_rope_kernel, "v_blocks": functools.partial(_proj_v_blocks_kernel, tk=tk)}[mode]
    if mode == "v_blocks":
        t_spec = pl.BlockSpec((N_GROUPS, tm // tk, LANES + ONES_ROWS, tk), lambda i: (0, i, 0, 0))
        t_shape = jax.ShapeDtypeStruct((N_GROUPS, rows // tk, LANES + ONES_ROWS, tk), BF16)
    out = pl.pallas_call(
        body,
        grid=(rows // tm,),
        in_specs=in_specs,
        out_specs=[f_spec, t_spec] if with_f32 else t_spec,
        out_shape=[f_shape, t_shape] if with_f32 else t_shape,
        compiler_params=_params("parallel"),
        name="proj_" + mode,
    )(*args)
    return (out[0], out[1]) if with_f32 else (None, out)


def _rope_tables(pos):
    half = HD_A // 2
    inv = ROPE_THETA ** (-jnp.arange(half, dtype=F32) / half)
    ang = pos.astype(F32)[:, None] * inv[None, :]
    cos, sin = jnp.cos(ang), jnp.sin(ang)
    reps = LANES // HD_A
    return (jnp.tile(jnp.concatenate([cos, cos], axis=-1), (1, reps)),
            jnp.tile(jnp.concatenate([-sin, sin], axis=-1), (1, reps)))


def _head_lanes(q, c):
    lane = lax.broadcasted_iota(jnp.int32, (1, LANES), 1)
    return jnp.where((lane >= c * HD_A) & (lane < (c + 1) * HD_A), q, jnp.zeros_like(q))


def _softmax_block(s, state, v, acc_ref):
    m, l = state
    m_new = jnp.maximum(m, jnp.max(s, axis=-1, keepdims=True))
    alpha = jnp.exp(m - m_new)
    p = jnp.exp(s - m_new)
    l_new = alpha * l + jnp.sum(p, axis=-1, keepdims=True)
    acc_ref[...] = alpha * acc_ref[...] + _dot(p.astype(BF16), v)
    return m_new, l_new


def _suffix_matrix(n):
    j = lax.broadcasted_iota(jnp.int32, (n, n), 0)
    s = lax.broadcasted_iota(jnp.int32, (n, n), 1)
    return jnp.where(j >= s, 1.0, 0.0).astype(BF16)


def _stick_block(z, mask, carry, v, acc_ref, suffix_mat):
    log_keep = jnp.minimum(-z, 0.0) - jnp.log(1.0 + jnp.exp(-jnp.abs(z)))
    if mask is not None:
        log_keep = jnp.where(mask, log_keep, 0.0)
    hi = log_keep.astype(BF16)
    lo = (log_keep - hi.astype(F32)).astype(BF16)
    suffix = _dot(hi, suffix_mat) + _dot(lo, suffix_mat)
    a = jnp.exp(z + suffix + carry)
    if mask is not None:
        a = jnp.where(mask, a, 0.0)
    acc_ref[...] += _dot(a.astype(BF16), v)
    return carry + suffix[:, 0:1]


def _lambda(lam_ref):
    lam = lam_ref[...]
    d1 = jnp.sum(lam[0:1] * lam[1:2], axis=-1, keepdims=True)
    d2 = jnp.sum(lam[2:3] * lam[3:4], axis=-1, keepdims=True)
    return jnp.exp(d1) - jnp.exp(d2) + LAM_INIT


def _diff_finish(o1, o2, lam, g_head):
    o = o1 - lam * o2
    o = o * lax.rsqrt(jnp.mean(o * o, axis=-1, keepdims=True) + EPS)
    return (o * g_head * (1.0 - LAM_INIT)).astype(BF16)


def _key_norm_bound(k_ref, c, seq, tq):
    rows = _tile(seq, 2048, 16)
    lane = lax.broadcasted_iota(jnp.int32, (8, LANES), 1)
    ones_c = jnp.where((lane >= c * HD_A) & (lane < (c + 1) * HD_A), 1.0, 0.0).astype(BF16)

    def body(i, best):
        k = k_ref[0, pl.ds(pl.multiple_of(i * rows, rows), rows), :].astype(F32)
        return jnp.maximum(best, _dot_nt(ones_c, (k * k).astype(BF16)))

    best = lax.fori_loop(0, seq // rows, body, jnp.zeros((8, rows), F32))
    return jnp.broadcast_to(jnp.max(best[0:1], axis=1, keepdims=True), (1, tq))


def _diff_prompt_kernel(q1_ref, q2_ref, k1_ref, k2_ref, vt_ref, lam_ref, g_ref, o_ref, acc_ref, s_ref, p_ref,
                        kmax_ref, *, tq, tk, seq):
    qi = pl.program_id(2)
    lam = _lambda(lam_ref)

    @pl.when(qi == 0)
    def _():
        for idx, (c, k_ref) in enumerate((c, k_ref) for c in range(2) for k_ref in (k1_ref, k2_ref)):
            kmax_ref[idx] = _key_norm_bound(k_ref, c, seq, tq)

    n_full = (qi * tq) // tk
    q_pos = qi * tq + lax.broadcasted_iota(jnp.int32, (1, tq), 1)
    limit = (q_pos // CHUNK + 1) * CHUNK
    k_pos = n_full * tk + lax.broadcasted_iota(jnp.int32, (tk, 1), 0)
    edge_mask = k_pos < limit

    streams = [(c, _head_lanes(q_ref[0], c), k_ref)
               for c in range(2) for q_ref, k_ref in ((q1_ref, k1_ref), (q2_ref, k2_ref))]
    n_streams = len(streams)

    def score_block(j):
        start = pl.multiple_of(j * tk, tk)
        return [_dot_nt(k_ref[0, pl.ds(start, tk), :], q) for _, q, k_ref in streams]

    def store_scores(scores):
        for idx, s in enumerate(scores):
            s_ref[idx] = s

    def value_block(j, alphas, slot=1):
        for idx, (c, _, _) in enumerate(streams):
            pv = _dot(vt_ref[c, j], p_ref[slot, idx])
            acc_ref[idx] = pv + (acc_ref[idx] if alphas is None else alphas[idx] * acc_ref[idx])

    def run(shift_block, init_state, rescale):
        acc_ref[...] = jnp.zeros_like(acc_ref)
        p_ref[...] = jnp.zeros_like(p_ref)
        store_scores(score_block(0))

        def step(j, carry):
            state, alphas = carry
            next_scores = score_block(j + 1)
            value_block(jnp.maximum(j - 1, 0), alphas if rescale else None)
            carry = shift_block(state, None)
            store_scores(next_scores)
            return carry

        state, alphas = lax.fori_loop(0, n_full, step, init_state)
        value_block(jnp.maximum(n_full - 1, 0), alphas if rescale else None)
        state, alphas = shift_block(state, edge_mask)
        value_block(n_full, alphas if rescale else None)

    def masked_scores(idx, mask):
        s = s_ref[idx]
        return s if mask is None else jnp.where(mask, s, NEG)

    ones8 = jnp.ones((8, LANES), BF16)
    bounds = []
    for idx, (_, q, _) in enumerate(streams):
        qf = q.astype(F32)
        q_sq = _dot_nt(ones8, (qf * qf).astype(BF16))[0:1]
        bounds.append(jnp.sqrt(q_sq * kmax_ref[idx]) * 1.02)

    def store_weights(scores, mask, slot):
        for idx, s in enumerate(scores):
            if mask is not None:
                s = jnp.where(mask, s, NEG)
            p_ref[slot, idx] = jnp.exp(s - bounds[idx]).astype(BF16)

    def bound_block(j, mask):
        scores = score_block(j)
        value_block(jnp.maximum(j - 1, 0), None)
        store_weights(scores, mask, 1)

    def bound_pair(i, _):
        j = 2 * i
        scores_a = score_block(j)
        value_block(jnp.maximum(j - 1, 0), None)
        scores_b = score_block(j + 1)
        store_weights(scores_a, None, 0)
        value_block(j, None, slot=0)
        store_weights(scores_b, None, 1)

    acc_ref[...] = jnp.zeros_like(acc_ref)
    p_ref[...] = jnp.zeros_like(p_ref)
    lax.fori_loop(0, n_full // 2, bound_pair, None)

    @pl.when(n_full % 2 == 1)
    def _():
        bound_block(n_full - 1, None)

    bound_block(n_full, edge_mask)
    value_block(n_full, None)

    denom_min = acc_ref[0, VD_A:VD_A + 1, :]
    for idx in range(1, n_streams):
        denom_min = jnp.minimum(denom_min, acc_ref[idx, VD_A:VD_A + 1, :])

    @pl.when(jnp.logical_not(jnp.min(denom_min) >= DENOM_MIN))
    def _():
        def max_block(state, mask):
            new_state, alphas = [], []
            for idx in range(n_streams):
                s = masked_scores(idx, mask)
                m_new = jnp.maximum(state[idx], jnp.max(s, axis=0, keepdims=True))
                alphas.append(jnp.exp(state[idx] - m_new))
                new_state.append(m_new)
                p_ref[1, idx] = jnp.exp(s - m_new).astype(BF16)
            return tuple(new_state), tuple(alphas)

        init = (tuple(jnp.full((1, tq), NEG, F32) for _ in streams), tuple(jnp.ones((1, tq), F32) for _ in streams))
        run(max_block, init, rescale=True)

    for c in range(2):
        a1, a2 = acc_ref[2 * c], acc_ref[2 * c + 1]
        o = a1[:VD_A] / a1[VD_A:VD_A + 1] - lam * (a2[:VD_A] / a2[VD_A:VD_A + 1])
        o = o * lax.rsqrt(jnp.mean(o * o, axis=0, keepdims=True) + EPS)
        o = o * g_ref[...] * (1.0 - LAM_INIT)
        o_ref[:, c * VD_A:(c + 1) * VD_A] = o.T.astype(BF16)


def _diff_prompt(qa_t, ka_t, va_tt, lam_vecs, g_head, batch, seq, tk):
    tq = _tile(tk, 256, CHUNK)
    nq = seq // tq
    half = N_GROUPS // 2
    q_spec = lambda off: pl.BlockSpec((1, tq, LANES), lambda b, p, i: (p + off, b * nq + i, 0))
    k_spec = lambda off: pl.BlockSpec((1, seq, LANES), lambda b, p, i: (p + off, b, 0))
    return pl.pallas_call(
        functools.partial(_diff_prompt_kernel, tq=tq, tk=tk, seq=seq),
        grid=(batch, half, nq),
        in_specs=[q_spec(0), q_spec(half), k_spec(0), k_spec(half),
                  pl.BlockSpec((2, seq // tk, VD_A + ONES_ROWS, tk), lambda b, p, i: (p, b, 0, 0)),
                  pl.BlockSpec((4, HD_A), lambda b, p, i: (0, 0)),
                  pl.BlockSpec((VD_A, 1), lambda b, p, i: (0, 0))],
        out_specs=pl.BlockSpec((tq, 2 * VD_A), lambda b, p, i: (b * nq + i, p)),
        out_shape=jax.ShapeDtypeStruct((batch * seq, SEG_W), BF16),
        scratch_shapes=[pltpu.VMEM((4, VD_A + ONES_ROWS, tq), F32), pltpu.VMEM((4, tk, tq), F32),
                        pltpu.VMEM((2, 4, tk, tq), BF16), pltpu.VMEM((4, 1, tq), F32)],
        compiler_params=_params("arbitrary", "arbitrary", "arbitrary"),
        name="diff_prompt",
    )(qa_t, qa_t, ka_t, ka_t, va_tt, lam_vecs, g_head.reshape(VD_A, 1))


def _stick_prompt_kernel(q_ref, k_ref, v_ref, o_ref, acc_ref, *, tq, tk):
    qi = pl.program_id(2)
    q = q_ref[0]
    suffix_mat = _suffix_matrix(tk)
    scale = HD_B ** -0.5
    acc_ref[...] = jnp.zeros_like(acc_ref)
    carry = jnp.zeros((tq, 1), F32)
    q_pos = qi * tq + lax.broadcasted_iota(jnp.int32, (tq, 1), 0)

    n_edge = tq // tk
    for e in range(n_edge - 1, -1, -1):
        start = pl.multiple_of(qi * tq + e * tk, tk)
        k_pos = start + lax.broadcasted_iota(jnp.int32, (1, tk), 1)
        z = _dot_nt(q, k_ref[0, pl.ds(start, tk), :]) * scale
        carry = _stick_block(z, k_pos < q_pos, carry, v_ref[0, pl.ds(start, tk), :], acc_ref, suffix_mat)

    n_full = (qi * tq) // tk

    def cond(st):
        i, carry = st
        return jnp.logical_and(i < n_full, jnp.max(carry) > UNDERFLOW_LOG)

    def body(st):
        i, carry = st
        start = pl.multiple_of((n_full - 1 - i) * tk, tk)
        z = _dot_nt(q, k_ref[0, pl.ds(start, tk), :]) * scale
        return i + 1, _stick_block(z, None, carry, v_ref[0, pl.ds(start, tk), :], acc_ref, suffix_mat)

    lax.while_loop(cond, body, (jnp.int32(0), carry))
    o_ref[...] = acc_ref[...].astype(BF16)


def _stick_prompt(qb_t, kb_t, vb_t, batch, seq):
    tq = _tile(seq, 512, 8)
    tk = _tile(tq, 256, 8)
    nq = seq // tq
    kv_spec = pl.BlockSpec((1, seq, LANES), lambda b, h, i: (h, b, 0))
    return pl.pallas_call(
        functools.partial(_stick_prompt_kernel, tq=tq, tk=tk),
        grid=(batch, HB, nq),
        in_specs=[pl.BlockSpec((1, tq, LANES), lambda b, h, i: (h, b * nq + i, 0)), kv_spec, kv_spec],
        out_specs=pl.BlockSpec((tq, HD_B), lambda b, h, i: (b * nq + i, h)),
        out_shape=jax.ShapeDtypeStruct((batch * seq, SEG_W), BF16),
        scratch_shapes=[pltpu.VMEM((tq, HD_B), F32)],
        compiler_params=_params("parallel", "parallel", "arbitrary"),
        name="stick_prompt",
    )(qb_t, kb_t, vb_t)


def _diff_sample_kernel(q1_ref, q2_ref, kn1_ref, kn2_ref, vn_ref, kc1_ref, kc2_ref, vc_ref, lam_ref, g_ref,
                        o_ref, acc_ref, *, ts, past):
    lam = _lambda(lam_ref)
    q_pos = past + lax.broadcasted_iota(jnp.int32, (ts, 1), 0)
    limit = (q_pos // CHUNK + 1) * CHUNK
    k_pos = past + lax.broadcasted_iota(jnp.int32, (1, ts), 1)
    new_mask = k_pos < limit

    for c in range(2):
        outs = []
        for q_ref, kn_ref, kc_ref in ((q1_ref, kn1_ref, kc1_ref), (q2_ref, kn2_ref, kc2_ref)):
            q = _head_lanes(q_ref[0], c)
            acc_ref[...] = jnp.zeros_like(acc_ref)
            state = (jnp.full((ts, 1), NEG, F32), jnp.zeros((ts, 1), F32))
            s = _dot_nt(q, kc_ref[0].astype(BF16))
            state = _softmax_block(s, state, vc_ref[0, :, c * VD_A:(c + 1) * VD_A].astype(BF16), acc_ref)
            s = jnp.where(new_mask, _dot_nt(q, kn_ref[0]), NEG)
            _, l = _softmax_block(s, state, vn_ref[c], acc_ref)
            outs.append(acc_ref[...] / l)
        o_ref[:, c * VD_A:(c + 1) * VD_A] = _diff_finish(outs[0], outs[1], lam, g_ref[...])


def _diff_sample(qa_t, ka_t, va_t, cache_k, cache_v, lam_vecs, g_head, batch, ts, past):
    half = N_GROUPS // 2
    new_spec = lambda off: pl.BlockSpec((1, ts, LANES), lambda b, p: (p + off, b, 0))
    cache_spec = lambda off: pl.BlockSpec((1, past, LANES), lambda b, p: (b, 0, p + off))
    return pl.pallas_call(
        functools.partial(_diff_sample_kernel, ts=ts, past=past),
        grid=(batch, half),
        in_specs=[new_spec(0), new_spec(half), new_spec(0), new_spec(half),
                  pl.BlockSpec((2, ts, LANES), lambda b, p: (p, b, 0)),
                  cache_spec(0), cache_spec(half),
                  pl.BlockSpec((1, past, 2 * VD_A), lambda b, p: (b, 0, p)),
                  pl.BlockSpec((4, HD_A), lambda b, p: (0, 0)),
                  pl.BlockSpec((1, VD_A), lambda b, p: (0, 0))],
        out_specs=pl.BlockSpec((ts, 2 * VD_A), lambda b, p: (b, p)),
        out_shape=jax.ShapeDtypeStruct((batch * ts, SEG_W), BF16),
        scratch_shapes=[pltpu.VMEM((ts, VD_A), F32)],
        compiler_params=_params("parallel", "parallel"),
        name="diff_sample",
    )(qa_t, qa_t, ka_t, ka_t, va_t, cache_k, cache_k, cache_v, lam_vecs, g_head)


def _stick_sample_kernel(q_ref, kn_ref, vn_ref, kc_ref, vc_ref, o_ref, acc_ref, *, ts, past, tk):
    q = q_ref[0]
    scale = HD_B ** -0.5
    acc_ref[...] = jnp.zeros_like(acc_ref)
    carry = jnp.zeros((ts, 1), F32)
    row = lax.broadcasted_iota(jnp.int32, (ts, 1), 0)
    col = lax.broadcasted_iota(jnp.int32, (1, ts), 1)
    z = _dot_nt(q, kn_ref[0]) * scale
    carry = _stick_block(z, col < row, carry, vn_ref[0], acc_ref, _suffix_matrix(ts))
    suffix_mat = _suffix_matrix(tk)
    n_full = past // tk

    def cond(st):
        i, carry = st
        return jnp.logical_and(i < n_full, jnp.max(carry) > UNDERFLOW_LOG)

    def body(st):
        i, carry = st
        start = pl.multiple_of((n_full - 1 - i) * tk, tk)
        k = kc_ref[0, pl.ds(start, tk), :].astype(BF16)
        v = vc_ref[0, pl.ds(start, tk), :].astype(BF16)
        return i + 1, _stick_block(_dot_nt(q, k) * scale, None, carry, v, acc_ref, suffix_mat)

    lax.while_loop(cond, body, (jnp.int32(0), carry))
    o_ref[...] = acc_ref[...].astype(BF16)


def _stick_sample(qb_t, kb_t, vb_t, cache_k, cache_v, batch, ts, past):
    tk = _tile(past, 256, 8)
    new_spec = pl.BlockSpec((1, ts, LANES), lambda b, h: (h, b, 0))
    cache_spec = pl.BlockSpec((1, past, LANES), lambda b, h: (b, 0, h))
    return pl.pallas_call(
        functools.partial(_stick_sample_kernel, ts=ts, past=past, tk=tk),
        grid=(batch, HB),
        in_specs=[new_spec, new_spec, new_spec, cache_spec, cache_spec],
        out_specs=pl.BlockSpec((ts, HD_B), lambda b, h: (b, h)),
        out_shape=jax.ShapeDtypeStruct((batch * ts, SEG_W), BF16),
        scratch_shapes=[pltpu.VMEM((ts, HD_B), F32)],
        compiler_params=_params("parallel", "parallel"),
        name="stick_sample",
    )(qb_t, kb_t, vb_t, cache_k, cache_v)


def _merge_kernel(h_ref, oa_ref, ob_ref, wga_ref, wgb_ref, wpa_ref, wpb_ref, o_ref):
    h = h_ref[...]
    ga = jax.nn.sigmoid(_dot(h, wga_ref[...]))
    gb = jax.nn.sigmoid(_dot(h, wgb_ref[...]))
    ya = _dot(oa_ref[...], wpa_ref[...])
    yb = _dot(ob_ref[...], wpb_ref[...])
    o_ref[...] = (ga * ya + gb * yb).astype(BF16)


def _merge(h, oa, ob, w_gate, w_proj_a, w_proj_b):
    rows, d = h.shape
    tm = _tile(rows, 512)
    tn = 512
    nn = d // tn
    row_spec = lambda w: pl.BlockSpec((tm, w), lambda i, n: (i, 0))
    return pl.pallas_call(
        _merge_kernel,
        grid=(rows // tm, nn),
        in_specs=[row_spec(d), row_spec(SEG_W), row_spec(SEG_W),
                  pl.BlockSpec((d, tn), lambda i, n: (0, n)),
                  pl.BlockSpec((d, tn), lambda i, n: (0, nn + n)),
                  pl.BlockSpec((SEG_W, tn), lambda i, n: (0, n)),
                  pl.BlockSpec((SEG_W, tn), lambda i, n: (0, n))],
        out_specs=pl.BlockSpec((tm, tn), lambda i, n: (i, n)),
        out_shape=jax.ShapeDtypeStruct((rows, d), BF16),
        compiler_params=_params("parallel", "arbitrary"),
        name="merge",
    )(h, oa, ob, w_gate, w_gate, w_proj_a, w_proj_b)


def _out_kernel(x_ref, m_ref, w_ref, g_post_ref, g_pre_ref, x1_ref, hm_ref):
    x1 = x_ref[...] + _rms(_dot(m_ref[...], w_ref[...]), g_post_ref[...])
    x1_ref[...] = x1
    hm_ref[...] = _rms(x1, g_pre_ref[...]).astype(BF16)


def _out_proj(x, merged, w_out, g_post_mix, g_pre_mlp):
    rows, d = x.shape
    tm = _tile(rows, 256)
    row = pl.BlockSpec((tm, d), lambda i: (i, 0))
    vec = pl.BlockSpec((1, d), lambda i: (0, 0))
    return pl.pallas_call(
        _out_kernel,
        grid=(rows // tm,),
        in_specs=[row, row, pl.BlockSpec((d, d), lambda i: (0, 0)), vec, vec],
        out_specs=[row, row],
        out_shape=[jax.ShapeDtypeStruct((rows, d), F32), jax.ShapeDtypeStruct((rows, d), BF16)],
        compiler_params=_params("parallel"),
        name="out_proj",
    )(x, merged, w_out, g_post_mix, g_pre_mlp)


def _mlp_kernel(x1_ref, hm_ref, wu_ref, wd_ref, g_ref, y_ref, acc_ref):
    f = pl.program_id(1)

    @pl.when(f == 0)
    def _():
        acc_ref[...] = jnp.zeros_like(acc_ref)

    u = jnp.square(jnp.maximum(_dot(hm_ref[...], wu_ref[...]), 0.0))
    acc_ref[...] += _dot(u.astype(BF16), wd_ref[...])

    @pl.when(f == pl.num_programs(1) - 1)
    def _():
        y_ref[...] = x1_ref[...] + _rms(acc_ref[...], g_ref[...])


def _mlp(x1, hm, w_up, w_down, g_post_mlp):
    rows, d = x1.shape
    d_ff = w_up.shape[1]
    tm = _tile(rows, 512)
    tf = 1024
    row = pl.BlockSpec((tm, d), lambda i, f: (i, 0))
    return pl.pallas_call(
        _mlp_kernel,
        grid=(rows // tm, d_ff // tf),
        in_specs=[row, row,
                  pl.BlockSpec((d, tf), lambda i, f: (0, f)),
                  pl.BlockSpec((tf, d), lambda i, f: (f, 0)),
                  pl.BlockSpec((1, d), lambda i, f: (0, 0))],
        out_specs=row,
        out_shape=jax.ShapeDtypeStruct((rows, d), F32),
        scratch_shapes=[pltpu.VMEM((tm, d), F32)],
        compiler_params=_params("parallel", "arbitrary"),
        name="mlp",
    )(x1, hm, w_up, w_down, g_post_mlp)


def _project_all(h, w_segs, cos, sin, v_block=None):
    _, qa_t = _project(h, w_segs[0], "q_rope", cos, sin)
    ka_f, ka_t = _project(h, w_segs[1], "k_rope", cos, sin)
    if v_block is None:
        va_f, va_t = _project(h, w_segs[2], "kv")
    else:
        va_f, va_t = _project(h, w_segs[2], "v_blocks", tk=v_block)
    _, qb_t = _project(h, w_segs[3], "q")
    kb_f, kb_t = _project(h, w_segs[4], "kv")
    vb_f, vb_t = _project(h, w_segs[5], "kv")
    return (qa_t, ka_t, va_t, qb_t, kb_t, vb_t), (ka_f, va_f, kb_f, vb_f)


def _tail(x, h, oa, ob, w):
    merged = _merge(h, oa, ob, w["gate"], w["proj_a"], w["proj_b"])
    x1, hm = _out_proj(x, merged, w["out"], w["g_post_mix"], w["g_pre_mlp"])
    return _mlp(x1, hm, w["up"], w["down"], w["g_post_mlp"])


def kernel(x_prompt, x_sample, cache_diff_k, cache_diff_v, cache_sb_k, cache_sb_v, g_pre_mix, w_in, lambda_q1, lambda_k1, lambda_q2, lambda_k2, g_diff_head, w_gate, w_proj_a, w_proj_b, w_out, g_post_mix, g_pre_mlp, w_up, w_down, g_post_mlp):
    bp, tp, d = x_prompt.shape
    bs, ts, _ = x_sample.shape
    depth, _, past = cache_diff_k.shape[:3]
    assert depth == 1 and tp % CHUNK == 0

    w_in_b = w_in[0].astype(BF16)
    w_segs = [w_in_b[:, s * SEG_W:(s + 1) * SEG_W] for s in range(6)]
    w = dict(gate=w_gate[0].astype(BF16), proj_a=w_proj_a[0].astype(BF16), proj_b=w_proj_b[0].astype(BF16),
             out=w_out[0].astype(BF16), up=w_up[0].astype(BF16), down=w_down[0].astype(BF16),
             g_post_mix=g_post_mix, g_pre_mlp=g_pre_mlp, g_post_mlp=g_post_mlp)
    lam_vecs = jnp.concatenate([lambda_q1, lambda_k1, lambda_q2, lambda_k2], axis=0).astype(F32)

    xp = x_prompt.reshape(bp * tp, d)
    hp = _prenorm(xp, g_pre_mix)
    cos_p, sin_p = _rope_tables(jnp.arange(tp, dtype=jnp.int32))
    tk_diff = _tile(tp, 256, CHUNK)
    (qa_t, ka_t, va_tt, qb_t, kb_t, vb_t), kv_p = _project_all(hp, w_segs, cos_p, sin_p, v_block=tk_diff)
    oa = _diff_prompt(qa_t, ka_t, va_tt, lam_vecs, g_diff_head, bp, tp, tk_diff)
    ob = _stick_prompt(qb_t, kb_t, vb_t, bp, tp)
    yp = _tail(xp, hp, oa, ob, w).reshape(bp, tp, d)

    xs = x_sample.reshape(bs * ts, d)
    hs = _prenorm(xs, g_pre_mix)
    pos_s = past + jnp.arange(ts, dtype=jnp.int32)
    cos_s, sin_s = _rope_tables(jnp.tile(pos_s, bs))
    (qa_t, ka_t, va_t, qb_t, kb_t, vb_t), kv_s = _project_all(hs, w_segs, cos_s, sin_s)
    oa = _diff_sample(qa_t, ka_t, va_t, cache_diff_k[0].reshape(bs, past, SEG_W),
                      cache_diff_v[0].reshape(bs, past, SEG_W), lam_vecs, g_diff_head, bs, ts, past)
    ob = _stick_sample(qb_t, kb_t, vb_t, cache_sb_k[0].reshape(bs, past, SEG_W),
                       cache_sb_v[0].reshape(bs, past, SEG_W), bs, ts, past)
    ys = _tail(xs, hs, oa, ob, w).reshape(bs, ts, d)

    def caches(kv, b, t):
        ka_f, va_f, kb_f, vb_f = kv
        return (ka_f.reshape(1, b, t, 2 * HA, HD_A), va_f.reshape(1, b, t, HA, VD_A),
                kb_f.reshape(1, b, t, HB, HD_B), vb_f.reshape(1, b, t, HB, HD_B))

    return (yp, ys) + caches(kv_p, bp, tp) + caches(kv_s, bs, ts)
```

```python
import functools
import math

import jax
import jax.numpy as jnp
from jax import lax
from jax.experimental import pallas as pl
from jax.experimental.pallas import tpu as pltpu

F32 = jnp.float32
BF16 = jnp.bfloat16

CHUNK = 64
HA = 8
HD_A = 64
VD_A = 128
HB = 8
HD_B = 128
SEG_W = 1024
LANES = 128
N_GROUPS = SEG_W // LANES
ROPE_THETA = 10000.0
EPS = 1e-6
NEG = -1e30
LAM_INIT = 0.8 - 0.6 * math.exp(-0.3 * 0)
UNDERFLOW_LOG = -104.0
DENOM_MIN = 1e-20
ONES_ROWS = 16
VMEM_LIMIT = 56 * 1024 * 1024


def _tile(n, pref, mult=8):
    if n <= pref:
        return n
    for t in range(pref, 0, -1):
        if n % t == 0 and t % mult == 0:
            return t
    raise ValueError(f"no tile for {n}")


def _params(*sem):
    return pltpu.CompilerParams(dimension_semantics=sem, vmem_limit_bytes=VMEM_LIMIT)


def _dot(a, b):
    return jnp.dot(a, b, preferred_element_type=F32)


def _dot_nt(a, b):
    return lax.dot_general(a, b, (((1,), (1,)), ((), ())), preferred_element_type=F32)


def _rms(x, g):
    return x * lax.rsqrt(jnp.mean(x * x, axis=-1, keepdims=True) + EPS) * g


def _prenorm_kernel(x_ref, g_ref, h_ref):
    h_ref[...] = _rms(x_ref[...], g_ref[...]).astype(BF16)


def _prenorm(x, g):
    rows, d = x.shape
    tm = _tile(rows, 512)
    return pl.pallas_call(
        _prenorm_kernel,
        grid=(rows // tm,),
        in_specs=[pl.BlockSpec((tm, d), lambda i: (i, 0)), pl.BlockSpec((1, d), lambda i: (0, 0))],
        out_specs=pl.BlockSpec((tm, d), lambda i: (i, 0)),
        out_shape=jax.ShapeDtypeStruct((rows, d), BF16),
        compiler_params=_params("parallel"),
        name="prenorm",
    )(x, g)


PROJ_CHUNK = 2


def _project_chunks(h_ref, w_ref):
    h = h_ref[...]
    for c0 in range(0, N_GROUPS, PROJ_CHUNK):
        yield c0, _dot(h, w_ref[:, c0 * LANES:(c0 + PROJ_CHUNK) * LANES])


def _groups(z):
    return [z[:, c * LANES:(c + 1) * LANES] for c in range(z.shape[1] // LANES)]


def _store_groups(t_ref, z, c0):
    for c, zc in enumerate(_groups(z)):
        t_ref[c0 + c] = zc.astype(BF16)


def _rope_groups(z, cos, sin):
    lane = lax.broadcasted_iota(jnp.int32, (1, LANES), 1)
    first_half = (lane % HD_A) < (HD_A // 2)
    out = []
    for zc in _groups(z):
        partner = jnp.where(first_half, pltpu.roll(zc, LANES - HD_A // 2, 1), pltpu.roll(zc, HD_A // 2, 1))
        out.append(zc * cos + partner * sin)
    return out


def _store_heads(f_ref, z, c0):
    for c, zc in enumerate(_groups(z)):
        f_ref[:, c0 + c, :] = zc


def _proj_kv_kernel(h_ref, w_ref, f_ref, t_ref):
    for c0, z in _project_chunks(h_ref, w_ref):
        _store_heads(f_ref, z, c0)
        _store_groups(t_ref, z, c0)


def _proj_v_blocks_kernel(h_ref, w_ref, f_ref, t_ref, *, tk):
    for c0, z in _project_chunks(h_ref, w_ref):
        _store_heads(f_ref, z, c0)
        for c, zc in enumerate(_groups(z)):
            zt = zc.T
            for j in range(zt.shape[1] // tk):
                t_ref[c0 + c, j, :LANES, :] = zt[:, j * tk:(j + 1) * tk].astype(BF16)
    row = lax.broadcasted_iota(jnp.int32, (ONES_ROWS, tk), 0)
    ones_rows = jnp.where(row == 0, 1.0, 0.0).astype(BF16)
    for c in range(N_GROUPS):
        for j in range(t_ref.shape[1]):
            t_ref[c, j, LANES:, :] = ones_rows


def _proj_q_kernel(h_ref, w_ref, t_ref):
    for c0, z in _project_chunks(h_ref, w_ref):
        _store_groups(t_ref, z, c0)


def _proj_k_rope_kernel(h_ref, w_ref, cos_ref, sin_ref, f_ref, t_ref):
    for c0, z in _project_chunks(h_ref, w_ref):
        for c, zc in enumerate(_rope_groups(z, cos_ref[...], sin_ref[...]), c0):
            f_ref[:, c * LANES:(c + 1) * LANES] = zc
            t_ref[c] = zc.astype(BF16)


def _proj_k_rope_t_kernel(h_ref, w_ref, cos_ref, sin_ref, f_ref, t_ref):
    for c0, z in _project_chunks(h_ref, w_ref):
        for c, zc in enumerate(_rope_groups(z, cos_ref[...], sin_ref[...]), c0):
            zt = zc.T
            f_ref[0, 2 * c] = zt[:HD_A]
            f_ref[0, 2 * c + 1] = zt[HD_A:]
            t_ref[c] = zc.astype(BF16)


def _proj_q_rope_kernel(h_ref, w_ref, cos_ref, sin_ref, t_ref):
    for c0, z in _project_chunks(h_ref, w_ref):
        for c, zc in enumerate(_rope_groups(z, cos_ref[...], sin_ref[...]), c0):
            t_ref[c] = (zc * (HD_A ** -0.5)).astype(BF16)


def _project(h, w, mode, cos=None, sin=None, tk=None):
    rows, d = h.shape
    tm = _tile(rows, 512)
    n_tab = None if cos is None else cos.shape[0] // tm
    in_specs = [pl.BlockSpec((tm, d), lambda i: (i, 0)), pl.BlockSpec((d, SEG_W), lambda i: (0, 0))]
    args = [h, w]
    if cos is not None:
        tab = pl.BlockSpec((tm, LANES), lambda i: (i % n_tab, 0))
        in_specs += [tab, tab]
        args += [cos, sin]
    t_spec = pl.BlockSpec((N_GROUPS, tm, LANES), lambda i: (0, i, 0))
    t_shape = jax.ShapeDtypeStruct((N_GROUPS, rows, LANES), BF16)
    with_f32 = mode in ("kv", "k_rope", "k_rope_t", "v_blocks")
    body = {"kv": _proj_kv_kernel, "q": _proj_q_kernel, "k_rope": _proj_k_rope_kernel,
            "k_rope_t": _proj_k_rope_t_kernel, "q_rope": _proj_q_rope_kernel,
            "v_blocks": functools.partial(_proj_v_blocks_kernel, tk=tk)}[mode]
    if mode == "k_rope":
        f_spec = pl.BlockSpec((tm, SEG_W), lambda i: (i, 0))
        f_shape = jax.ShapeDtypeStruct((rows, SEG_W), F32)
    elif mode == "k_rope_t":
        f_spec = pl.BlockSpec((1, 2 * HA, HD_A, tm), lambda i: (i // n_tab, 0, 0, i % n_tab))
        f_shape = jax.ShapeDtypeStruct((rows // tk, 2 * HA, HD_A, tk), F32)
    else:
        f_spec = pl.BlockSpec((tm, N_GROUPS, LANES), lambda i: (i, 0, 0))
        f_shape = jax.ShapeDtypeStruct((rows, N_GROUPS, LANES), F32)
    if mode == "v_blocks":
        t_spec = pl.BlockSpec((N_GROUPS, tm // tk, LANES + ONES_ROWS, tk), lambda i: (0, i, 0, 0))
        t_shape = jax.ShapeDtypeStruct((N_GROUPS, rows // tk, LANES + ONES_ROWS, tk), BF16)
    out = pl.pallas_call(
        body,
        grid=(rows // tm,),
        in_specs=in_specs,
        out_specs=[f_spec, t_spec] if with_f32 else t_spec,
        out_shape=[f_shape, t_shape] if with_f32 else t_shape,
        compiler_params=_params("parallel"),
        name="proj_" + mode,
    )(*args)
    return (out[0], out[1]) if with_f32 else (None, out)


def _rope_tables(pos):
    half = HD_A // 2
    inv = ROPE_THETA ** (-jnp.arange(half, dtype=F32) / half)
    ang = pos.astype(F32)[:, None] * inv[None, :]
    cos, sin = jnp.cos(ang), jnp.sin(ang)
    reps = LANES // HD_A
    return (jnp.tile(jnp.concatenate([cos, cos], axis=-1), (1, reps)),
            jnp.tile(jnp.concatenate([-sin, sin], axis=-1), (1, reps)))


def _head_lanes(q, c):
    lane = lax.broadcasted_iota(jnp.int32, (1, LANES), 1)
    return jnp.where((lane >= c * HD_A) & (lane < (c + 1) * HD_A), q, jnp.zeros_like(q))


def _suffix_matrix(n):
    j = lax.broadcasted_iota(jnp.int32, (n, n), 0)
    s = lax.broadcasted_iota(jnp.int32, (n, n), 1)
    return jnp.where(j >= s, 1.0, 0.0).astype(BF16)


def _stick_block(z, mask, carry, v, acc_ref, suffix_mat):
    log_keep = jnp.minimum(-z, 0.0) - jnp.log(1.0 + jnp.exp(-jnp.abs(z)))
    if mask is not None:
        log_keep = jnp.where(mask, log_keep, 0.0)
    hi = log_keep.astype(BF16)
    lo = (log_keep - hi.astype(F32)).astype(BF16)
    suffix = _dot(hi, suffix_mat) + _dot(lo, suffix_mat)
    a = jnp.exp(z + suffix + carry)
    if mask is not None:
        a = jnp.where(mask, a, 0.0)
    acc_ref[...] += _dot(a.astype(BF16), v)
    return carry + suffix[:, 0:1]


def _lambda(lam_ref):
    lam = lam_ref[...]
    d1 = jnp.sum(lam[0:1] * lam[1:2], axis=-1, keepdims=True)
    d2 = jnp.sum(lam[2:3] * lam[3:4], axis=-1, keepdims=True)
    return jnp.exp(d1) - jnp.exp(d2) + LAM_INIT


def _diff_finish(o1, o2, lam, g_head):
    o = o1 - lam * o2
    o = o * lax.rsqrt(jnp.mean(o * o, axis=-1, keepdims=True) + EPS)
    return (o * g_head * (1.0 - LAM_INIT)).astype(BF16)


def _key_norm_bound(k_ref, c, seq, tq):
    rows = _tile(seq, 2048, 16)
    lane = lax.broadcasted_iota(jnp.int32, (8, LANES), 1)
    ones_c = jnp.where((lane >= c * HD_A) & (lane < (c + 1) * HD_A), 1.0, 0.0).astype(BF16)

    def body(i, best):
        k = k_ref[0, pl.ds(pl.multiple_of(i * rows, rows), rows), :].astype(F32)
        return jnp.maximum(best, _dot_nt(ones_c, (k * k).astype(BF16)))

    best = lax.fori_loop(0, seq // rows, body, jnp.zeros((8, rows), F32))
    return jnp.broadcast_to(jnp.max(best[0:1], axis=1, keepdims=True), (1, tq))


def _diff_prompt_kernel(q1_ref, q2_ref, k1_ref, k2_ref, vt_ref, lam_ref, g_ref, o_ref, acc_ref, s_ref, p_ref,
                        kmax_ref, *, tq, tk, seq):
    qi = pl.program_id(2)
    lam = _lambda(lam_ref)

    @pl.when(qi == 0)
    def _():
        for idx, (c, k_ref) in enumerate((c, k_ref) for c in range(2) for k_ref in (k1_ref, k2_ref)):
            kmax_ref[idx] = _key_norm_bound(k_ref, c, seq, tq)

    n_full = (qi * tq) // tk
    q_pos = qi * tq + lax.broadcasted_iota(jnp.int32, (1, tq), 1)
    limit = (q_pos // CHUNK + 1) * CHUNK
    k_pos = n_full * tk + lax.broadcasted_iota(jnp.int32, (tk, 1), 0)
    edge_mask = k_pos < limit

    streams = [(c, _head_lanes(q_ref[0], c), k_ref)
               for c in range(2) for q_ref, k_ref in ((q1_ref, k1_ref), (q2_ref, k2_ref))]
    n_streams = len(streams)

    def score_block(j):
        start = pl.multiple_of(j * tk, tk)
        return [_dot_nt(k_ref[0, pl.ds(start, tk), :], q) for _, q, k_ref in streams]

    def store_scores(scores):
        for idx, s in enumerate(scores):
            s_ref[idx] = s

    def value_block(j, alphas, slot=1):
        for idx, (c, _, _) in enumerate(streams):
            pv = _dot(vt_ref[c, j], p_ref[slot, idx])
            acc_ref[idx] = pv + (acc_ref[idx] if alphas is None else alphas[idx] * acc_ref[idx])

    def run(shift_block, init_state, rescale):
        acc_ref[...] = jnp.zeros_like(acc_ref)
        p_ref[...] = jnp.zeros_like(p_ref)
        store_scores(score_block(0))

        def step(j, carry):
            state, alphas = carry
            next_scores = score_block(j + 1)
            value_block(jnp.maximum(j - 1, 0), alphas if rescale else None)
            carry = shift_block(state, None)
            store_scores(next_scores)
            return carry

        state, alphas = lax.fori_loop(0, n_full, step, init_state)
        value_block(jnp.maximum(n_full - 1, 0), alphas if rescale else None)
        state, alphas = shift_block(state, edge_mask)
        value_block(n_full, alphas if rescale else None)

    def masked_scores(idx, mask):
        s = s_ref[idx]
        return s if mask is None else jnp.where(mask, s, NEG)

    ones8 = jnp.ones((8, LANES), BF16)
    bounds = []
    for idx, (_, q, _) in enumerate(streams):
        qf = q.astype(F32)
        q_sq = _dot_nt(ones8, (qf * qf).astype(BF16))[0:1]
        bounds.append(jnp.sqrt(q_sq * kmax_ref[idx]) * 1.02)

    def store_weights(scores, mask, slot):
        for idx, s in enumerate(scores):
            if mask is not None:
                s = jnp.where(mask, s, NEG)
            p_ref[slot, idx] = jnp.exp(s - bounds[idx]).astype(BF16)

    def bound_block(j, mask):
        scores = score_block(j)
        value_block(jnp.maximum(j - 1, 0), None)
        store_weights(scores, mask, 1)

    def bound_pair(i, _):
        j = 2 * i
        scores_a = score_block(j)
        value_block(jnp.maximum(j - 1, 0), None)
        scores_b = score_block(j + 1)
        store_weights(scores_a, None, 0)
        value_block(j, None, slot=0)
        store_weights(scores_b, None, 1)

    acc_ref[...] = jnp.zeros_like(acc_ref)
    p_ref[...] = jnp.zeros_like(p_ref)
    lax.fori_loop(0, n_full // 2, bound_pair, None)

    @pl.when(n_full % 2 == 1)
    def _():
        bound_block(n_full - 1, None)

    bound_block(n_full, edge_mask)
    value_block(n_full, None)

    denom_min = acc_ref[0, VD_A:VD_A + 1, :]
    for idx in range(1, n_streams):
        denom_min = jnp.minimum(denom_min, acc_ref[idx, VD_A:VD_A + 1, :])

    @pl.when(jnp.logical_not(jnp.min(denom_min) >= DENOM_MIN))
    def _():
        def max_block(state, mask):
            new_state, alphas = [], []
            for idx in range(n_streams):
                s = masked_scores(idx, mask)
                m_new = jnp.maximum(state[idx], jnp.max(s, axis=0, keepdims=True))
                alphas.append(jnp.exp(state[idx] - m_new))
                new_state.append(m_new)
                p_ref[1, idx] = jnp.exp(s - m_new).astype(BF16)
            return tuple(new_state), tuple(alphas)

        init = (tuple(jnp.full((1, tq), NEG, F32) for _ in streams), tuple(jnp.ones((1, tq), F32) for _ in streams))
        run(max_block, init, rescale=True)

    for c in range(2):
        a1, a2 = acc_ref[2 * c], acc_ref[2 * c + 1]
        o = a1[:VD_A] / a1[VD_A:VD_A + 1] - lam * (a2[:VD_A] / a2[VD_A:VD_A + 1])
        o = o * lax.rsqrt(jnp.mean(o * o, axis=0, keepdims=True) + EPS)
        o = o * g_ref[...] * (1.0 - LAM_INIT)
        o_ref[:, c * VD_A:(c + 1) * VD_A] = o.T.astype(BF16)


def _diff_prompt(qa_t, ka_t, va_tt, lam_vecs, g_head, batch, seq, tk):
    tq = _tile(tk, 256, CHUNK)
    nq = seq // tq
    half = N_GROUPS // 2
    q_spec = lambda off: pl.BlockSpec((1, tq, LANES), lambda b, p, i: (p + off, b * nq + i, 0))
    k_spec = lambda off: pl.BlockSpec((1, seq, LANES), lambda b, p, i: (p + off, b, 0))
    return pl.pallas_call(
        functools.partial(_diff_prompt_kernel, tq=tq, tk=tk, seq=seq),
        grid=(batch, half, nq),
        in_specs=[q_spec(0), q_spec(half), k_spec(0), k_spec(half),
                  pl.BlockSpec((2, seq // tk, VD_A + ONES_ROWS, tk), lambda b, p, i: (p, b, 0, 0)),
                  pl.BlockSpec((4, HD_A), lambda b, p, i: (0, 0)),
                  pl.BlockSpec((VD_A, 1), lambda b, p, i: (0, 0))],
        out_specs=pl.BlockSpec((tq, 2 * VD_A), lambda b, p, i: (b * nq + i, p)),
        out_shape=jax.ShapeDtypeStruct((batch * seq, SEG_W), BF16),
        scratch_shapes=[pltpu.VMEM((4, VD_A + ONES_ROWS, tq), F32), pltpu.VMEM((4, tk, tq), F32),
                        pltpu.VMEM((2, 4, tk, tq), BF16), pltpu.VMEM((4, 1, tq), F32)],
        compiler_params=_params("arbitrary", "arbitrary", "arbitrary"),
        name="diff_prompt",
    )(qa_t, qa_t, ka_t, ka_t, va_tt, lam_vecs, g_head.reshape(VD_A, 1))


def _stick_blocks(zs, mask, carries, vs, acc_ref, suffix_mat):
    parts = []
    for z in zs:
        log_keep = jnp.minimum(-z, 0.0) - jnp.log(1.0 + jnp.exp(-jnp.abs(z)))
        if mask is not None:
            log_keep = jnp.where(mask, log_keep, 0.0)
        hi = log_keep.astype(BF16)
        parts.append((hi, (log_keep - hi.astype(F32)).astype(BF16)))
    suffixes = [_dot(hi, suffix_mat) + _dot(lo, suffix_mat) for hi, lo in parts]
    weights = []
    for z, suffix, carry in zip(zs, suffixes, carries):
        a = jnp.exp(z + suffix + carry)
        if mask is not None:
            a = jnp.where(mask, a, 0.0)
        weights.append(a.astype(BF16))
    for idx, (a, v) in enumerate(zip(weights, vs)):
        acc_ref[idx] += _dot(a, v)
    return tuple(carry + suffix[:, 0:1] for carry, suffix in zip(carries, suffixes))


def _stick_prompt_kernel(q_ref, k_ref, v_ref, o_ref, acc_ref, *, tq, tk, heads):
    qi = pl.program_id(2)
    qs = [q_ref[h] for h in range(heads)]
    suffix_mat = _suffix_matrix(tk)
    scale = HD_B ** -0.5
    acc_ref[...] = jnp.zeros_like(acc_ref)
    q_pos = qi * tq + lax.broadcasted_iota(jnp.int32, (tq, 1), 0)

    def block(start, mask, carries):
        zs = [_dot_nt(qs[h], k_ref[h, pl.ds(start, tk), :]) * scale for h in range(heads)]
        vs = [v_ref[h, pl.ds(start, tk), :] for h in range(heads)]
        return _stick_blocks(zs, mask, carries, vs, acc_ref, suffix_mat)

    carries = tuple(jnp.zeros((tq, 1), F32) for _ in range(heads))
    for e in range(tq // tk - 1, -1, -1):
        start = pl.multiple_of(qi * tq + e * tk, tk)
        k_pos = start + lax.broadcasted_iota(jnp.int32, (1, tk), 1)
        carries = block(start, k_pos < q_pos, carries)

    n_full = (qi * tq) // tk

    def cond(st):
        i, carries = st
        largest = functools.reduce(jnp.maximum, [jnp.max(c) for c in carries])
        return jnp.logical_and(i < n_full, largest > UNDERFLOW_LOG)

    def body(st):
        i, carries = st
        return i + 1, block(pl.multiple_of((n_full - 1 - i) * tk, tk), None, carries)

    lax.while_loop(cond, body, (jnp.int32(0), carries))
    for h in range(heads):
        o_ref[:, h * HD_B:(h + 1) * HD_B] = acc_ref[h].astype(BF16)


def _stick_prompt(qb_t, kb_t, vb_t, batch, seq):
    tq = _tile(seq, 256, 8)
    tk = tq
    heads = 4
    nq = seq // tq
    kv_spec = pl.BlockSpec((heads, seq, LANES), lambda b, h, i: (h, b, 0), pipeline_mode=pl.Buffered(1))
    return pl.pallas_call(
        functools.partial(_stick_prompt_kernel, tq=tq, tk=tk, heads=heads),
        grid=(batch, HB // heads, nq),
        in_specs=[pl.BlockSpec((heads, tq, LANES), lambda b, h, i: (h, b * nq + i, 0)), kv_spec, kv_spec],
        out_specs=pl.BlockSpec((tq, heads * HD_B), lambda b, h, i: (b * nq + i, h)),
        out_shape=jax.ShapeDtypeStruct((batch * seq, SEG_W), BF16),
        scratch_shapes=[pltpu.VMEM((heads, tq, HD_B), F32)],
        compiler_params=_params("parallel", "parallel", "arbitrary"),
        name="stick_prompt",
    )(qb_t, kb_t, vb_t)


def _diff_sample_kernel(qa_ref, ka_ref, va_ref, kc_ref, vc_ref, lam_ref, g_ref, o_ref, *, ts, past):
    lam = _lambda(lam_ref)
    q_pos = past + lax.broadcasted_iota(jnp.int32, (ts, 1), 0)
    limit = (q_pos // CHUNK + 1) * CHUNK
    k_pos = past + lax.broadcasted_iota(jnp.int32, (1, ts), 1)
    new_mask = k_pos < limit

    def attend(g, c, v_cache, v_new):
        q = _head_lanes(qa_ref[g], c)
        k_cache = jnp.concatenate([kc_ref[0, 2 * g].astype(BF16), kc_ref[0, 2 * g + 1].astype(BF16)], axis=0)
        s_cache = _dot(q, k_cache)
        s_new = jnp.where(new_mask, _dot_nt(q, ka_ref[g]), NEG)
        m = jnp.maximum(jnp.max(s_cache, axis=-1, keepdims=True), jnp.max(s_new, axis=-1, keepdims=True))
        p_cache = jnp.exp(s_cache - m)
        p_new = jnp.exp(s_new - m)
        l = jnp.sum(p_cache, axis=-1, keepdims=True) + jnp.sum(p_new, axis=-1, keepdims=True)
        return (_dot(p_cache.astype(BF16), v_cache) + _dot(p_new.astype(BF16), v_new)) / l

    half = N_GROUPS // 2
    for h in range(HA):
        g, c = h // 2, h % 2
        v_cache = vc_ref[0, :, h, :].astype(BF16)
        o1 = attend(g, c, v_cache, va_ref[h])
        o2 = attend(g + half, c, v_cache, va_ref[h])
        o_ref[:, h * VD_A:(h + 1) * VD_A] = _diff_finish(o1, o2, lam, g_ref[...])


def _diff_sample(qa_t, ka_t, va_t, cache_k, cache_v, lam_vecs, g_head, batch, ts, past):
    new_spec = pl.BlockSpec((N_GROUPS, ts, LANES), lambda b: (0, b, 0))
    return pl.pallas_call(
        functools.partial(_diff_sample_kernel, ts=ts, past=past),
        grid=(batch,),
        in_specs=[new_spec, new_spec, new_spec,
                  pl.BlockSpec((1, 2 * HA, HD_A, past), lambda b: (b, 0, 0, 0)),
                  pl.BlockSpec((1, past, HA, VD_A), lambda b: (b, 0, 0, 0)),
                  pl.BlockSpec((4, HD_A), lambda b: (0, 0)),
                  pl.BlockSpec((1, VD_A), lambda b: (0, 0))],
        out_specs=pl.BlockSpec((ts, SEG_W), lambda b: (b, 0)),
        out_shape=jax.ShapeDtypeStruct((batch * ts, SEG_W), BF16),
        compiler_params=_params("parallel"),
        name="diff_sample",
    )(qa_t, ka_t, va_t, cache_k, cache_v, lam_vecs, g_head)


def _stick_sample_kernel(q_ref, kn_ref, vn_ref, kc_ref, vc_ref, o_ref, acc_ref, *, ts, past, tk):
    scale = HD_B ** -0.5
    row = lax.broadcasted_iota(jnp.int32, (ts, 1), 0)
    col = lax.broadcasted_iota(jnp.int32, (1, ts), 1)
    new_suffix_mat = _suffix_matrix(ts)
    suffix_mat = _suffix_matrix(tk)
    n_full = past // tk

    for h in range(HB):
        q = q_ref[h]
        acc_ref[...] = jnp.zeros_like(acc_ref)
        z = _dot_nt(q, kn_ref[h]) * scale
        carry = _stick_block(z, col < row, jnp.zeros((ts, 1), F32), vn_ref[h], acc_ref, new_suffix_mat)

        def cond(st):
            i, carry = st
            return jnp.logical_and(i < n_full, jnp.max(carry) > UNDERFLOW_LOG)

        def body(st, q=q, h=h):
            i, carry = st
            start = pl.multiple_of((n_full - 1 - i) * tk, tk)
            k = kc_ref[0, pl.ds(start, tk), h, :].astype(BF16)
            v = vc_ref[0, pl.ds(start, tk), h, :].astype(BF16)
            return i + 1, _stick_block(_dot_nt(q, k) * scale, None, carry, v, acc_ref, suffix_mat)

        lax.while_loop(cond, body, (jnp.int32(0), carry))
        o_ref[:, h * HD_B:(h + 1) * HD_B] = acc_ref[...].astype(BF16)


def _stick_sample(qb_t, kb_t, vb_t, cache_k, cache_v, batch, ts, past):
    tk = _tile(past, 256, 8)
    new_spec = pl.BlockSpec((N_GROUPS, ts, LANES), lambda b: (0, b, 0))
    cache_spec = pl.BlockSpec((1, past, HB, HD_B), lambda b: (b, 0, 0, 0))
    return pl.pallas_call(
        functools.partial(_stick_sample_kernel, ts=ts, past=past, tk=tk),
        grid=(batch,),
        in_specs=[new_spec, new_spec, new_spec, cache_spec, cache_spec],
        out_specs=pl.BlockSpec((ts, SEG_W), lambda b: (b, 0)),
        out_shape=jax.ShapeDtypeStruct((batch * ts, SEG_W), BF16),
        scratch_shapes=[pltpu.VMEM((ts, HD_B), F32)],
        compiler_params=_params("parallel"),
        name="stick_sample",
    )(qb_t, kb_t, vb_t, cache_k, cache_v)


def _merge_kernel(h_ref, oa_ref, ob_ref, wga_ref, wgb_ref, wpa_ref, wpb_ref, o_ref):
    h = h_ref[...]
    ga = jax.nn.sigmoid(_dot(h, wga_ref[...]))
    gb = jax.nn.sigmoid(_dot(h, wgb_ref[...]))
    ya = _dot(oa_ref[...], wpa_ref[...])
    yb = _dot(ob_ref[...], wpb_ref[...])
    o_ref[...] = (ga * ya + gb * yb).astype(BF16)


def _merge(h, oa, ob, w_gate, w_proj_a, w_proj_b):
    rows, d = h.shape
    tm = _tile(rows, 512)
    tn = 512
    nn = d // tn
    row_spec = lambda w: pl.BlockSpec((tm, w), lambda i, n: (i, 0))
    return pl.pallas_call(
        _merge_kernel,
        grid=(rows // tm, nn),
        in_specs=[row_spec(d), row_spec(SEG_W), row_spec(SEG_W),
                  pl.BlockSpec((d, tn), lambda i, n: (0, n)),
                  pl.BlockSpec((d, tn), lambda i, n: (0, nn + n)),
                  pl.BlockSpec((SEG_W, tn), lambda i, n: (0, n)),
                  pl.BlockSpec((SEG_W, tn), lambda i, n: (0, n))],
        out_specs=pl.BlockSpec((tm, tn), lambda i, n: (i, n)),
        out_shape=jax.ShapeDtypeStruct((rows, d), BF16),
        compiler_params=_params("parallel", "arbitrary"),
        name="merge",
    )(h, oa, ob, w_gate, w_gate, w_proj_a, w_proj_b)


def _out_kernel(x_ref, m_ref, w_ref, g_post_ref, g_pre_ref, x1_ref, hm_ref):
    x1 = x_ref[...] + _rms(_dot(m_ref[...], w_ref[...]), g_post_ref[...])
    x1_ref[...] = x1
    hm_ref[...] = _rms(x1, g_pre_ref[...]).astype(BF16)


def _out_proj(x, merged, w_out, g_post_mix, g_pre_mlp):
    rows, d = x.shape
    tm = _tile(rows, 256)
    row = pl.BlockSpec((tm, d), lambda i: (i, 0))
    vec = pl.BlockSpec((1, d), lambda i: (0, 0))
    return pl.pallas_call(
        _out_kernel,
        grid=(rows // tm,),
        in_specs=[row, row, pl.BlockSpec((d, d), lambda i: (0, 0)), vec, vec],
        out_specs=[row, row],
        out_shape=[jax.ShapeDtypeStruct((rows, d), F32), jax.ShapeDtypeStruct((rows, d), BF16)],
        compiler_params=_params("parallel"),
        name="out_proj",
    )(x, merged, w_out, g_post_mix, g_pre_mlp)


def _mlp_kernel(x1_ref, hm_ref, wu_ref, wd_ref, g_ref, y_ref, acc_ref):
    f = pl.program_id(1)

    @pl.when(f == 0)
    def _():
        acc_ref[...] = jnp.zeros_like(acc_ref)

    hm = hm_ref[...]
    half = wu_ref.shape[1] // 2
    down = []
    for s in range(2):
        u = jnp.square(jnp.maximum(_dot(hm, wu_ref[:, s * half:(s + 1) * half]), 0.0))
        down.append(_dot(u.astype(BF16), wd_ref[s * half:(s + 1) * half, :]))
    acc_ref[...] += down[0] + down[1]

    @pl.when(f == pl.num_programs(1) - 1)
    def _():
        y_ref[...] = x1_ref[...] + _rms(acc_ref[...], g_ref[...])


def _mlp(x1, hm, w_up, w_down, g_post_mlp):
    rows, d = x1.shape
    d_ff = w_up.shape[1]
    tm = _tile(rows, 512)
    tf = 1024
    row = pl.BlockSpec((tm, d), lambda i, f: (i, 0))
    return pl.pallas_call(
        _mlp_kernel,
        grid=(rows // tm, d_ff // tf),
        in_specs=[row, row,
                  pl.BlockSpec((d, tf), lambda i, f: (0, f)),
                  pl.BlockSpec((tf, d), lambda i, f: (f, 0)),
                  pl.BlockSpec((1, d), lambda i, f: (0, 0))],
        out_specs=row,
        out_shape=jax.ShapeDtypeStruct((rows, d), F32),
        scratch_shapes=[pltpu.VMEM((tm, d), F32)],
        compiler_params=_params("parallel", "arbitrary"),
        name="mlp",
    )(x1, hm, w_up, w_down, g_post_mlp)


def _project_all(h, w_segs, cos, sin, v_block=None):
    _, qa_t = _project(h, w_segs[0], "q_rope", cos, sin)
    if v_block is None:
        ka_f, ka_t = _project(h, w_segs[1], "k_rope", cos, sin)
    else:
        ka_f, ka_t = _project(h, w_segs[1], "k_rope_t", cos, sin, tk=cos.shape[0])
    if v_block is None:
        va_f, va_t = _project(h, w_segs[2], "kv")
    else:
        va_f, va_t = _project(h, w_segs[2], "v_blocks", tk=v_block)
    _, qb_t = _project(h, w_segs[3], "q")
    kb_f, kb_t = _project(h, w_segs[4], "kv")
    vb_f, vb_t = _project(h, w_segs[5], "kv")
    return (qa_t, ka_t, va_t, qb_t, kb_t, vb_t), (ka_f, va_f, kb_f, vb_f)


def _tail(x, h, oa, ob, w):
    merged = _merge(h, oa, ob, w["gate"], w["proj_a"], w["proj_b"])
    x1, hm = _out_proj(x, merged, w["out"], w["g_post_mix"], w["g_pre_mlp"])
    return _mlp(x1, hm, w["up"], w["down"], w["g_post_mlp"])


def kernel(x_prompt, x_sample, cache_diff_k, cache_diff_v, cache_sb_k, cache_sb_v, g_pre_mix, w_in, lambda_q1, lambda_k1, lambda_q2, lambda_k2, g_diff_head, w_gate, w_proj_a, w_proj_b, w_out, g_post_mix, g_pre_mlp, w_up, w_down, g_post_mlp):
    bp, tp, d = x_prompt.shape
    bs, ts, _ = x_sample.shape
    depth, _, past = cache_diff_k.shape[:3]
    assert depth == 1 and tp % CHUNK == 0

    w_in_b = w_in[0].astype(BF16)
    w_segs = [w_in_b[:, s * SEG_W:(s + 1) * SEG_W] for s in range(6)]
    w = dict(gate=w_gate[0].astype(BF16), proj_a=w_proj_a[0].astype(BF16), proj_b=w_proj_b[0].astype(BF16),
             out=w_out[0].astype(BF16), up=w_up[0].astype(BF16), down=w_down[0].astype(BF16),
             g_post_mix=g_post_mix, g_pre_mlp=g_pre_mlp, g_post_mlp=g_post_mlp)
    lam_vecs = jnp.concatenate([lambda_q1, lambda_k1, lambda_q2, lambda_k2], axis=0).astype(F32)

    xp = x_prompt.reshape(bp * tp, d)
    hp = _prenorm(xp, g_pre_mix)
    cos_p, sin_p = _rope_tables(jnp.arange(tp, dtype=jnp.int32))
    tk_diff = _tile(tp, 256, CHUNK)
    (qa_t, ka_t, va_tt, qb_t, kb_t, vb_t), kv_p = _project_all(hp, w_segs, cos_p, sin_p, v_block=tk_diff)
    oa = _diff_prompt(qa_t, ka_t, va_tt, lam_vecs, g_diff_head, bp, tp, tk_diff)
    ob = _stick_prompt(qb_t, kb_t, vb_t, bp, tp)
    yp = _tail(xp, hp, oa, ob, w).reshape(bp, tp, d)

    xs = x_sample.reshape(bs * ts, d)
    hs = _prenorm(xs, g_pre_mix)
    pos_s = past + jnp.arange(ts, dtype=jnp.int32)
    cos_s, sin_s = _rope_tables(jnp.tile(pos_s, bs))
    (qa_t, ka_t, va_t, qb_t, kb_t, vb_t), kv_s = _project_all(hs, w_segs, cos_s, sin_s)
    oa = _diff_sample(qa_t, ka_t, va_t, jnp.transpose(cache_diff_k[0], (0, 2, 3, 1)), cache_diff_v[0],
                      lam_vecs, g_diff_head, bs, ts, past)
    ob = _stick_sample(qb_t, kb_t, vb_t, cache_sb_k[0], cache_sb_v[0], bs, ts, past)
    ys = _tail(xs, hs, oa, ob, w).reshape(bs, ts, d)

    def caches(kv, b, t):
        ka_f, va_f, kb_f, vb_f = kv
        if ka_f.ndim == 4:
            ka_f = jnp.transpose(ka_f, (0, 3, 1, 2))
        return (ka_f.reshape(1, b, t, 2 * HA, HD_A), va_f.reshape(1, b, t, HA, VD_A),
                kb_f.reshape(1, b, t, HB, HD_B), vb_f.reshape(1, b, t, HB, HD_B))

    return (yp, ys) + caches(kv_p, bp, tp) + caches(kv_s, bs, ts)
```

```python
import functools
import math

import jax
import jax.numpy as jnp
from jax import lax
from jax.experimental import pallas as pl
from jax.experimental.pallas import tpu as pltpu

F32 = jnp.float32
BF16 = jnp.bfloat16

CHUNK = 64
HA = 8
HD_A = 64
VD_A = 128
HB = 8
HD_B = 128
SEG_W = 1024
LANES = 128
N_GROUPS = SEG_W // LANES
ROPE_THETA = 10000.0
EPS = 1e-6
NEG = -1e30
LAM_INIT = 0.8 - 0.6 * math.exp(-0.3 * 0)
UNDERFLOW_LOG = -104.0
DENOM_MIN = 1e-20
ONES_ROWS = 16
VMEM_LIMIT = 56 * 1024 * 1024


def _tile(n, pref, mult=8):
    if n <= pref:
        return n
    for t in range(pref, 0, -1):
        if n % t == 0 and t % mult == 0:
            return t
    raise ValueError(f"no tile for {n}")


def _params(*sem):
    return pltpu.CompilerParams(dimension_semantics=sem, vmem_limit_bytes=VMEM_LIMIT)


def _dot(a, b):
    return jnp.dot(a, b, preferred_element_type=F32)


def _dot_nt(a, b):
    return lax.dot_general(a, b, (((1,), (1,)), ((), ())), preferred_element_type=F32)


def _rms(x, g):
    return x * lax.rsqrt(jnp.mean(x * x, axis=-1, keepdims=True) + EPS) * g


def _prenorm_kernel(x_ref, g_ref, h_ref):
    h_ref[...] = _rms(x_ref[...], g_ref[...]).astype(BF16)


def _prenorm(x, g):
    rows, d = x.shape
    tm = _tile(rows, 512)
    return pl.pallas_call(
        _prenorm_kernel,
        grid=(rows // tm,),
        in_specs=[pl.BlockSpec((tm, d), lambda i: (i, 0)), pl.BlockSpec((1, d), lambda i: (0, 0))],
        out_specs=pl.BlockSpec((tm, d), lambda i: (i, 0)),
        out_shape=jax.ShapeDtypeStruct((rows, d), BF16),
        compiler_params=_params("parallel"),
        name="prenorm",
    )(x, g)


PROJ_CHUNK = 2


def _project_chunks(h_ref, w_ref):
    h = h_ref[...]
    for c0 in range(0, N_GROUPS, PROJ_CHUNK):
        yield c0, _dot(h, w_ref[:, c0 * LANES:(c0 + PROJ_CHUNK) * LANES])


def _groups(z):
    return [z[:, c * LANES:(c + 1) * LANES] for c in range(z.shape[1] // LANES)]


def _store_groups(t_ref, z, c0):
    for c, zc in enumerate(_groups(z)):
        t_ref[c0 + c] = zc.astype(BF16)


def _rope_groups(z, cos, sin):
    lane = lax.broadcasted_iota(jnp.int32, (1, LANES), 1)
    first_half = (lane % HD_A) < (HD_A // 2)
    out = []
    for zc in _groups(z):
        partner = jnp.where(first_half, pltpu.roll(zc, LANES - HD_A // 2, 1), pltpu.roll(zc, HD_A // 2, 1))
        out.append(zc * cos + partner * sin)
    return out


def _store_heads(f_ref, z, c0):
    for c, zc in enumerate(_groups(z)):
        f_ref[:, c0 + c, :] = zc


def _proj_kv_kernel(h_ref, w_ref, f_ref, t_ref):
    for c0, z in _project_chunks(h_ref, w_ref):
        _store_heads(f_ref, z, c0)
        _store_groups(t_ref, z, c0)


def _proj_v_blocks_kernel(h_ref, w_ref, f_ref, t_ref, *, tk):
    for c0, z in _project_chunks(h_ref, w_ref):
        _store_heads(f_ref, z, c0)
        for c, zc in enumerate(_groups(z)):
            zt = zc.T
            for j in range(zt.shape[1] // tk):
                t_ref[c0 + c, j, :LANES, :] = zt[:, j * tk:(j + 1) * tk].astype(BF16)
    row = lax.broadcasted_iota(jnp.int32, (ONES_ROWS, tk), 0)
    ones_rows = jnp.where(row == 0, 1.0, 0.0).astype(BF16)
    for c in range(N_GROUPS):
        for j in range(t_ref.shape[1]):
            t_ref[c, j, LANES:, :] = ones_rows


def _proj_q_kernel(h_ref, w_ref, t_ref):
    for c0, z in _project_chunks(h_ref, w_ref):
        _store_groups(t_ref, z, c0)


def _proj_k_rope_kernel(h_ref, w_ref, cos_ref, sin_ref, f_ref, t_ref):
    for c0, z in _project_chunks(h_ref, w_ref):
        for c, zc in enumerate(_rope_groups(z, cos_ref[...], sin_ref[...]), c0):
            f_ref[:, c * LANES:(c + 1) * LANES] = zc
            t_ref[c] = zc.astype(BF16)


def _proj_k_rope_t_kernel(h_ref, w_ref, cos_ref, sin_ref, f_ref, t_ref):
    for c0, z in _project_chunks(h_ref, w_ref):
        for c, zc in enumerate(_rope_groups(z, cos_ref[...], sin_ref[...]), c0):
            zt = zc.T
            f_ref[0, 2 * c] = zt[:HD_A]
            f_ref[0, 2 * c + 1] = zt[HD_A:]
            t_ref[c] = zc.astype(BF16)


def _proj_q_rope_kernel(h_ref, w_ref, cos_ref, sin_ref, t_ref):
    for c0, z in _project_chunks(h_ref, w_ref):
        for c, zc in enumerate(_rope_groups(z, cos_ref[...], sin_ref[...]), c0):
            t_ref[c] = (zc * (HD_A ** -0.5)).astype(BF16)


def _project(h, w, mode, cos=None, sin=None, tk=None):
    rows, d = h.shape
    tm = _tile(rows, 512)
    n_tab = None if cos is None else cos.shape[0] // tm
    in_specs = [pl.BlockSpec((tm, d), lambda i: (i, 0)), pl.BlockSpec((d, SEG_W), lambda i: (0, 0))]
    args = [h, w]
    if cos is not None:
        tab = pl.BlockSpec((tm, LANES), lambda i: (i % n_tab, 0))
        in_specs += [tab, tab]
        args += [cos, sin]
    t_spec = pl.BlockSpec((N_GROUPS, tm, LANES), lambda i: (0, i, 0))
    t_shape = jax.ShapeDtypeStruct((N_GROUPS, rows, LANES), BF16)
    with_f32 = mode in ("kv", "k_rope", "k_rope_t", "v_blocks")
    body = {"kv": _proj_kv_kernel, "q": _proj_q_kernel, "k_rope": _proj_k_rope_kernel,
            "k_rope_t": _proj_k_rope_t_kernel, "q_rope": _proj_q_rope_kernel,
            "v_blocks": functools.partial(_proj_v_blocks_kernel, tk=tk)}[mode]
    if mode == "k_rope":
        f_spec = pl.BlockSpec((tm, SEG_W), lambda i: (i, 0))
        f_shape = jax.ShapeDtypeStruct((rows, SEG_W), F32)
    elif mode == "k_rope_t":
        f_spec = pl.BlockSpec((1, 2 * HA, HD_A, tm), lambda i: (i // n_tab, 0, 0, i % n_tab))
        f_shape = jax.ShapeDtypeStruct((rows // tk, 2 * HA, HD_A, tk), F32)
    else:
        f_spec = pl.BlockSpec((tm, N_GROUPS, LANES), lambda i: (i, 0, 0))
        f_shape = jax.ShapeDtypeStruct((rows, N_GROUPS, LANES), F32)
    if mode == "v_blocks":
        t_spec = pl.BlockSpec((N_GROUPS, tm // tk, LANES + ONES_ROWS, tk), lambda i: (0, i, 0, 0))
        t_shape = jax.ShapeDtypeStruct((N_GROUPS, rows // tk, LANES + ONES_ROWS, tk), BF16)
    out = pl.pallas_call(
        body,
        grid=(rows // tm,),
        in_specs=in_specs,
        out_specs=[f_spec, t_spec] if with_f32 else t_spec,
        out_shape=[f_shape, t_shape] if with_f32 else t_shape,
        compiler_params=_params("parallel"),
        name="proj_" + mode,
    )(*args)
    return (out[0], out[1]) if with_f32 else (None, out)


def _rope_tables(pos):
    half = HD_A // 2
    inv = ROPE_THETA ** (-jnp.arange(half, dtype=F32) / half)
    ang = pos.astype(F32)[:, None] * inv[None, :]
    cos, sin = jnp.cos(ang), jnp.sin(ang)
    reps = LANES // HD_A
    return (jnp.tile(jnp.concatenate([cos, cos], axis=-1), (1, reps)),
            jnp.tile(jnp.concatenate([-sin, sin], axis=-1), (1, reps)))


def _head_lanes(q, c):
    lane = lax.broadcasted_iota(jnp.int32, (1, LANES), 1)
    return jnp.where((lane >= c * HD_A) & (lane < (c + 1) * HD_A), q, jnp.zeros_like(q))


def _suffix_matrix(n):
    j = lax.broadcasted_iota(jnp.int32, (n, n), 0)
    s = lax.broadcasted_iota(jnp.int32, (n, n), 1)
    return jnp.where(j >= s, 1.0, 0.0).astype(BF16)


def _stick_block(z, mask, carry, v, acc_ref, suffix_mat):
    log_keep = jnp.minimum(-z, 0.0) - jnp.log(1.0 + jnp.exp(-jnp.abs(z)))
    if mask is not None:
        log_keep = jnp.where(mask, log_keep, 0.0)
    hi = log_keep.astype(BF16)
    lo = (log_keep - hi.astype(F32)).astype(BF16)
    suffix = _dot(hi, suffix_mat) + _dot(lo, suffix_mat)
    a = jnp.exp(z + suffix + carry)
    if mask is not None:
        a = jnp.where(mask, a, 0.0)
    acc_ref[...] += _dot(a.astype(BF16), v)
    return carry + suffix[:, 0:1]


def _lambda(lam_ref):
    lam = lam_ref[...]
    d1 = jnp.sum(lam[0:1] * lam[1:2], axis=-1, keepdims=True)
    d2 = jnp.sum(lam[2:3] * lam[3:4], axis=-1, keepdims=True)
    return jnp.exp(d1) - jnp.exp(d2) + LAM_INIT


def _diff_finish(o1, o2, lam, g_head):
    o = o1 - lam * o2
    o = o * lax.rsqrt(jnp.mean(o * o, axis=-1, keepdims=True) + EPS)
    return (o * g_head * (1.0 - LAM_INIT)).astype(BF16)


def _key_norm_bound(k_ref, c, seq, tq):
    rows = _tile(seq, 2048, 16)
    lane = lax.broadcasted_iota(jnp.int32, (8, LANES), 1)
    ones_c = jnp.where((lane >= c * HD_A) & (lane < (c + 1) * HD_A), 1.0, 0.0).astype(BF16)

    def body(i, best):
        k = k_ref[0, pl.ds(pl.multiple_of(i * rows, rows), rows), :].astype(F32)
        return jnp.maximum(best, _dot_nt(ones_c, (k * k).astype(BF16)))

    best = lax.fori_loop(0, seq // rows, body, jnp.zeros((8, rows), F32))
    return jnp.broadcast_to(jnp.max(best[0:1], axis=1, keepdims=True), (1, tq))


def _diff_prompt_kernel(q1_ref, q2_ref, k1_ref, k2_ref, vt_ref, lam_ref, g_ref, o_ref, acc_ref, s_ref, p_ref,
                        kmax_ref, *, tq, tk, seq):
    qi = pl.program_id(2)
    lam = _lambda(lam_ref)

    @pl.when(qi == 0)
    def _():
        for idx, (c, k_ref) in enumerate((c, k_ref) for c in range(2) for k_ref in (k1_ref, k2_ref)):
            kmax_ref[idx] = _key_norm_bound(k_ref, c, seq, tq)

    n_full = (qi * tq) // tk
    n_edge = tq // tk
    q_pos = qi * tq + lax.broadcasted_iota(jnp.int32, (1, tq), 1)
    limit = (q_pos // CHUNK + 1) * CHUNK
    k_pos = n_full * tk + lax.broadcasted_iota(jnp.int32, (tk, 1), 0)
    edge_masks = [k_pos + e * tk < limit for e in range(n_edge)]

    streams = [(c, _head_lanes(q_ref[0], c), k_ref)
               for c in range(2) for q_ref, k_ref in ((q1_ref, k1_ref), (q2_ref, k2_ref))]
    n_streams = len(streams)

    def score_block(j):
        start = pl.multiple_of(j * tk, tk)
        return [_dot_nt(k_ref[0, pl.ds(start, tk), :], q) for _, q, k_ref in streams]

    def after(x):
        return 0.0 * x

    def store_scores(scores, readers=None):
        for idx, s in enumerate(scores):
            s_ref[idx] = s if readers is None else s + readers[idx]

    def value_block(j, alphas, slot=1):
        done = []
        for idx, (c, _, _) in enumerate(streams):
            pv = _dot(vt_ref[c, j], p_ref[slot, idx])
            acc_ref[idx] = pv + (acc_ref[idx] if alphas is None else alphas[idx] * acc_ref[idx])
            done.append(after(pv[VD_A:VD_A + 1, :]))
        return done

    def masked(s, mask):
        return s if mask is None else jnp.where(mask, s, NEG)

    ones8 = jnp.ones((8, LANES), BF16)
    bounds = []
    for idx, (_, q, _) in enumerate(streams):
        qf = q.astype(F32)
        q_sq = _dot_nt(ones8, (qf * qf).astype(BF16))[0:1]
        bounds.append(jnp.sqrt(q_sq * kmax_ref[idx]) * 1.02)

    def store_weights(scores, mask, slot, readers=None):
        for idx, s in enumerate(scores):
            shift = bounds[idx] if readers is None else bounds[idx] + readers[idx]
            p_ref[slot, idx] = jnp.exp(masked(s, mask) - shift).astype(BF16)

    def bound_block(j, mask):
        scores = score_block(j)
        done = value_block(jnp.maximum(j - 1, 0), None)
        store_weights(scores, mask, 1, done)

    def bound_pair(i, _):
        j = 2 * i
        scores_a = score_block(j)
        done_prev = value_block(jnp.maximum(j - 1, 0), None)
        scores_b = score_block(j + 1)
        store_weights(scores_a, None, 0)
        value_block(j, None, slot=0)
        store_weights(scores_b, None, 1, done_prev)

    acc_ref[...] = jnp.zeros_like(acc_ref)
    p_ref[...] = jnp.zeros_like(p_ref)
    lax.fori_loop(0, n_full // 2, bound_pair, None)

    @pl.when(n_full % 2 == 1)
    def _():
        bound_block(n_full - 1, None)

    for e, mask in enumerate(edge_masks):
        bound_block(n_full + e, mask)
    value_block(n_full + n_edge - 1, None)

    denom_min = acc_ref[0, VD_A:VD_A + 1, :]
    for idx in range(1, n_streams):
        denom_min = jnp.minimum(denom_min, acc_ref[idx, VD_A:VD_A + 1, :])

    @pl.when(jnp.logical_not(jnp.min(denom_min) >= DENOM_MIN))
    def _():
        def max_block(maxima, mask, done):
            new_maxima, alphas = [], []
            for idx in range(n_streams):
                s = masked(s_ref[idx], mask)
                m_new = jnp.maximum(maxima[idx], jnp.max(s, axis=0, keepdims=True))
                alphas.append(jnp.exp(maxima[idx] - m_new))
                new_maxima.append(m_new)
                p_ref[1, idx] = jnp.exp(s - (m_new + done[idx])).astype(BF16)
            return tuple(new_maxima), tuple(alphas)

        def step(j, carry):
            maxima, alphas = carry
            next_scores = score_block(j + 1)
            done = value_block(jnp.maximum(j - 1, 0), alphas)
            maxima, alphas = max_block(maxima, None, done)
            store_scores(next_scores, [after(m) for m in maxima])
            return maxima, alphas

        acc_ref[...] = jnp.zeros_like(acc_ref)
        p_ref[...] = jnp.zeros_like(p_ref)
        store_scores(score_block(0))
        maxima = tuple(jnp.full((1, tq), NEG, F32) for _ in streams)
        alphas = tuple(jnp.ones((1, tq), F32) for _ in streams)
        maxima, alphas = lax.fori_loop(0, n_full, step, (maxima, alphas))
        done = value_block(jnp.maximum(n_full - 1, 0), alphas)
        for e, mask in enumerate(edge_masks):
            if e > 0:
                store_scores(score_block(n_full + e), [after(m) for m in maxima])
            maxima, alphas = max_block(maxima, mask, done)
            done = value_block(n_full + e, alphas)

    for c in range(2):
        a1, a2 = acc_ref[2 * c], acc_ref[2 * c + 1]
        o = a1[:VD_A] / a1[VD_A:VD_A + 1] - lam * (a2[:VD_A] / a2[VD_A:VD_A + 1])
        o = o * lax.rsqrt(jnp.mean(o * o, axis=0, keepdims=True) + EPS)
        o = o * g_ref[...] * (1.0 - LAM_INIT)
        o_ref[:, c * VD_A:(c + 1) * VD_A] = o.T.astype(BF16)


def _diff_prompt(qa_t, ka_t, va_tt, lam_vecs, g_head, batch, seq, tk):
    tq = 2 * tk if seq % (2 * tk) == 0 else tk
    nq = seq // tq
    half = N_GROUPS // 2
    q_spec = lambda off: pl.BlockSpec((1, tq, LANES), lambda b, p, i: (p + off, b * nq + i, 0))
    k_spec = lambda off: pl.BlockSpec((1, seq, LANES), lambda b, p, i: (p + off, b, 0))
    return pl.pallas_call(
        functools.partial(_diff_prompt_kernel, tq=tq, tk=tk, seq=seq),
        grid=(batch, half, nq),
        in_specs=[q_spec(0), q_spec(half), k_spec(0), k_spec(half),
                  pl.BlockSpec((2, seq // tk, VD_A + ONES_ROWS, tk), lambda b, p, i: (p, b, 0, 0)),
                  pl.BlockSpec((4, HD_A), lambda b, p, i: (0, 0)),
                  pl.BlockSpec((VD_A, 1), lambda b, p, i: (0, 0))],
        out_specs=pl.BlockSpec((tq, 2 * VD_A), lambda b, p, i: (b * nq + i, p)),
        out_shape=jax.ShapeDtypeStruct((batch * seq, SEG_W), BF16),
        scratch_shapes=[pltpu.VMEM((4, VD_A + ONES_ROWS, tq), F32), pltpu.VMEM((4, tk, tq), F32),
                        pltpu.VMEM((2, 4, tk, tq), BF16), pltpu.VMEM((4, 1, tq), F32)],
        compiler_params=_params("arbitrary", "arbitrary", "arbitrary"),
        name="diff_prompt",
    )(qa_t, qa_t, ka_t, ka_t, va_tt, lam_vecs, g_head.reshape(VD_A, 1))


def _stick_blocks(zs, mask, carries, vs, acc_ref, suffix_mat):
    parts = []
    for z in zs:
        log_keep = jnp.minimum(-z, 0.0) - jnp.log(1.0 + jnp.exp(-jnp.abs(z)))
        if mask is not None:
            log_keep = jnp.where(mask, log_keep, 0.0)
        hi = log_keep.astype(BF16)
        parts.append((hi, (log_keep - hi.astype(F32)).astype(BF16)))
    suffixes = [_dot(hi, suffix_mat) + _dot(lo, suffix_mat) for hi, lo in parts]
    weights = []
    for z, suffix, carry in zip(zs, suffixes, carries):
        a = jnp.exp(z + suffix + carry)
        if mask is not None:
            a = jnp.where(mask, a, 0.0)
        weights.append(a.astype(BF16))
    for idx, (a, v) in enumerate(zip(weights, vs)):
        acc_ref[idx] += _dot(a, v)
    return tuple(carry + suffix[:, 0:1] for carry, suffix in zip(carries, suffixes))


def _stick_prompt_kernel(q_ref, k_ref, v_ref, o_ref, acc_ref, *, tq, tk, heads):
    qi = pl.program_id(2)
    qs = [q_ref[h] for h in range(heads)]
    suffix_mat = _suffix_matrix(tk)
    scale = HD_B ** -0.5
    acc_ref[...] = jnp.zeros_like(acc_ref)
    q_pos = qi * tq + lax.broadcasted_iota(jnp.int32, (tq, 1), 0)

    def block(start, mask, carries):
        zs = [_dot_nt(qs[h], k_ref[h, pl.ds(start, tk), :]) * scale for h in range(heads)]
        vs = [v_ref[h, pl.ds(start, tk), :] for h in range(heads)]
        return _stick_blocks(zs, mask, carries, vs, acc_ref, suffix_mat)

    carries = tuple(jnp.zeros((tq, 1), F32) for _ in range(heads))
    for e in range(tq // tk - 1, -1, -1):
        start = pl.multiple_of(qi * tq + e * tk, tk)
        k_pos = start + lax.broadcasted_iota(jnp.int32, (1, tk), 1)
        carries = block(start, k_pos < q_pos, carries)

    n_full = (qi * tq) // tk

    def cond(st):
        i, carries = st
        largest = functools.reduce(jnp.maximum, [jnp.max(c) for c in carries])
        return jnp.logical_and(i < n_full, largest > UNDERFLOW_LOG)

    def body(st):
        i, carries = st
        return i + 1, block(pl.multiple_of((n_full - 1 - i) * tk, tk), None, carries)

    lax.while_loop(cond, body, (jnp.int32(0), carries))
    for h in range(heads):
        o_ref[:, h * HD_B:(h + 1) * HD_B] = acc_ref[h].astype(BF16)


def _stick_prompt(qb_t, kb_t, vb_t, batch, seq):
    tq = _tile(seq, 256, 8)
    tk = tq
    heads = 4
    nq = seq // tq
    kv_spec = pl.BlockSpec((heads, seq, LANES), lambda b, h, i: (h, b, 0), pipeline_mode=pl.Buffered(1))
    return pl.pallas_call(
        functools.partial(_stick_prompt_kernel, tq=tq, tk=tk, heads=heads),
        grid=(batch, HB // heads, nq),
        in_specs=[pl.BlockSpec((heads, tq, LANES), lambda b, h, i: (h, b * nq + i, 0)), kv_spec, kv_spec],
        out_specs=pl.BlockSpec((tq, heads * HD_B), lambda b, h, i: (b * nq + i, h)),
        out_shape=jax.ShapeDtypeStruct((batch * seq, SEG_W), BF16),
        scratch_shapes=[pltpu.VMEM((heads, tq, HD_B), F32)],
        compiler_params=_params("parallel", "parallel", "arbitrary"),
        name="stick_prompt",
    )(qb_t, kb_t, vb_t)


def _diff_sample_kernel(qa_ref, ka_ref, va_ref, kc_ref, vc_ref, lam_ref, g_ref, o_ref, *, ts, past):
    lam = _lambda(lam_ref)
    q_pos = past + lax.broadcasted_iota(jnp.int32, (ts, 1), 0)
    limit = (q_pos // CHUNK + 1) * CHUNK
    k_pos = past + lax.broadcasted_iota(jnp.int32, (1, ts), 1)
    new_mask = k_pos < limit

    def attend(g, c, v_cache, v_new):
        q = _head_lanes(qa_ref[g], c)
        k_cache = jnp.concatenate([kc_ref[0, 2 * g].astype(BF16), kc_ref[0, 2 * g + 1].astype(BF16)], axis=0)
        s_cache = _dot(q, k_cache)
        s_new = jnp.where(new_mask, _dot_nt(q, ka_ref[g]), NEG)
        m = jnp.maximum(jnp.max(s_cache, axis=-1, keepdims=True), jnp.max(s_new, axis=-1, keepdims=True))
        p_cache = jnp.exp(s_cache - m)
        p_new = jnp.exp(s_new - m)
        l = jnp.sum(p_cache, axis=-1, keepdims=True) + jnp.sum(p_new, axis=-1, keepdims=True)
        return (_dot(p_cache.astype(BF16), v_cache) + _dot(p_new.astype(BF16), v_new)) / l

    half = N_GROUPS // 2
    for h in range(HA):
        g, c = h // 2, h % 2
        v_cache = vc_ref[0, :, h, :].astype(BF16)
        o1 = attend(g, c, v_cache, va_ref[h])
        o2 = attend(g + half, c, v_cache, va_ref[h])
        o_ref[:, h * VD_A:(h + 1) * VD_A] = _diff_finish(o1, o2, lam, g_ref[...])


def _diff_sample(qa_t, ka_t, va_t, cache_k, cache_v, lam_vecs, g_head, batch, ts, past):
    new_spec = pl.BlockSpec((N_GROUPS, ts, LANES), lambda b: (0, b, 0))
    return pl.pallas_call(
        functools.partial(_diff_sample_kernel, ts=ts, past=past),
        grid=(batch,),
        in_specs=[new_spec, new_spec, new_spec,
                  pl.BlockSpec((1, 2 * HA, HD_A, past), lambda b: (b, 0, 0, 0)),
                  pl.BlockSpec((1, past, HA, VD_A), lambda b: (b, 0, 0, 0)),
                  pl.BlockSpec((4, HD_A), lambda b: (0, 0)),
                  pl.BlockSpec((1, VD_A), lambda b: (0, 0))],
        out_specs=pl.BlockSpec((ts, SEG_W), lambda b: (b, 0)),
        out_shape=jax.ShapeDtypeStruct((batch * ts, SEG_W), BF16),
        compiler_params=_params("parallel"),
        name="diff_sample",
    )(qa_t, ka_t, va_t, cache_k, cache_v, lam_vecs, g_head)


def _stick_sample_kernel(q_ref, kn_ref, vn_ref, kc_ref, vc_ref, o_ref, acc_ref, *, ts, past, tk):
    scale = HD_B ** -0.5
    row = lax.broadcasted_iota(jnp.int32, (ts, 1), 0)
    col = lax.broadcasted_iota(jnp.int32, (1, ts), 1)
    new_suffix_mat = _suffix_matrix(ts)
    suffix_mat = _suffix_matrix(tk)
    n_full = past // tk

    for h in range(HB):
        q = q_ref[h]
        acc_ref[...] = jnp.zeros_like(acc_ref)
        z = _dot_nt(q, kn_ref[h]) * scale
        carry = _stick_block(z, col < row, jnp.zeros((ts, 1), F32), vn_ref[h], acc_ref, new_suffix_mat)

        def cond(st):
            i, carry = st
            return jnp.logical_and(i < n_full, jnp.max(carry) > UNDERFLOW_LOG)

        def body(st, q=q, h=h):
            i, carry = st
            start = pl.multiple_of((n_full - 1 - i) * tk, tk)
            k = kc_ref[0, pl.ds(start, tk), h, :].astype(BF16)
            v = vc_ref[0, pl.ds(start, tk), h, :].astype(BF16)
            return i + 1, _stick_block(_dot_nt(q, k) * scale, None, carry, v, acc_ref, suffix_mat)

        lax.while_loop(cond, body, (jnp.int32(0), carry))
        o_ref[:, h * HD_B:(h + 1) * HD_B] = acc_ref[...].astype(BF16)


def _stick_sample(qb_t, kb_t, vb_t, cache_k, cache_v, batch, ts, past):
    tk = _tile(past, 256, 8)
    new_spec = pl.BlockSpec((N_GROUPS, ts, LANES), lambda b: (0, b, 0))
    cache_spec = pl.BlockSpec((1, past, HB, HD_B), lambda b: (b, 0, 0, 0))
    return pl.pallas_call(
        functools.partial(_stick_sample_kernel, ts=ts, past=past, tk=tk),
        grid=(batch,),
        in_specs=[new_spec, new_spec, new_spec, cache_spec, cache_spec],
        out_specs=pl.BlockSpec((ts, SEG_W), lambda b: (b, 0)),
        out_shape=jax.ShapeDtypeStruct((batch * ts, SEG_W), BF16),
        scratch_shapes=[pltpu.VMEM((ts, HD_B), F32)],
        compiler_params=_params("parallel"),
        name="stick_sample",
    )(qb_t, kb_t, vb_t, cache_k, cache_v)


def _merge_kernel(h_ref, oa_ref, ob_ref, wga_ref, wgb_ref, wpa_ref, wpb_ref, o_ref):
    h = h_ref[...]
    ga = jax.nn.sigmoid(_dot(h, wga_ref[...]))
    gb = jax.nn.sigmoid(_dot(h, wgb_ref[...]))
    ya = _dot(oa_ref[...], wpa_ref[...])
    yb = _dot(ob_ref[...], wpb_ref[...])
    o_ref[...] = (ga * ya + gb * yb).astype(BF16)


def _merge(h, oa, ob, w_gate, w_proj_a, w_proj_b):
    rows, d = h.shape
    tm = _tile(rows, 512)
    tn = 512
    nn = d // tn
    row_spec = lambda w: pl.BlockSpec((tm, w), lambda i, n: (i, 0))
    return pl.pallas_call(
        _merge_kernel,
        grid=(rows // tm, nn),
        in_specs=[row_spec(d), row_spec(SEG_W), row_spec(SEG_W),
                  pl.BlockSpec((d, tn), lambda i, n: (0, n)),
                  pl.BlockSpec((d, tn), lambda i, n: (0, nn + n)),
                  pl.BlockSpec((SEG_W, tn), lambda i, n: (0, n)),
                  pl.BlockSpec((SEG_W, tn), lambda i, n: (0, n))],
        out_specs=pl.BlockSpec((tm, tn), lambda i, n: (i, n)),
        out_shape=jax.ShapeDtypeStruct((rows, d), BF16),
        compiler_params=_params("parallel", "arbitrary"),
        name="merge",
    )(h, oa, ob, w_gate, w_gate, w_proj_a, w_proj_b)


def _out_kernel(x_ref, m_ref, w_ref, g_post_ref, g_pre_ref, x1_ref, hm_ref):
    x1 = x_ref[...] + _rms(_dot(m_ref[...], w_ref[...]), g_post_ref[...])
    x1_ref[...] = x1
    hm_ref[...] = _rms(x1, g_pre_ref[...]).astype(BF16)


def _out_proj(x, merged, w_out, g_post_mix, g_pre_mlp):
    rows, d = x.shape
    tm = _tile(rows, 256)
    row = pl.BlockSpec((tm, d), lambda i: (i, 0))
    vec = pl.BlockSpec((1, d), lambda i: (0, 0))
    return pl.pallas_call(
        _out_kernel,
        grid=(rows // tm,),
        in_specs=[row, row, pl.BlockSpec((d, d), lambda i: (0, 0)), vec, vec],
        out_specs=[row, row],
        out_shape=[jax.ShapeDtypeStruct((rows, d), F32), jax.ShapeDtypeStruct((rows, d), BF16)],
        compiler_params=_params("parallel"),
        name="out_proj",
    )(x, merged, w_out, g_post_mix, g_pre_mlp)


def _mlp_kernel(x1_ref, hm_ref, wu_ref, wd_ref, g_ref, y_ref, acc_ref):
    f = pl.program_id(1)

    @pl.when(f == 0)
    def _():
        acc_ref[...] = jnp.zeros_like(acc_ref)

    hm = hm_ref[...]
    half = wu_ref.shape[1] // 2
    down = []
    for s in range(2):
        u = jnp.square(jnp.maximum(_dot(hm, wu_ref[:, s * half:(s + 1) * half]), 0.0))
        down.append(_dot(u.astype(BF16), wd_ref[s * half:(s + 1) * half, :]))
    acc_ref[...] += down[0] + down[1]

    @pl.when(f == pl.num_programs(1) - 1)
    def _():
        y_ref[...] = x1_ref[...] + _rms(acc_ref[...], g_ref[...])


def _mlp(x1, hm, w_up, w_down, g_post_mlp):
    rows, d = x1.shape
    d_ff = w_up.shape[1]
    tm = _tile(rows, 512)
    tf = 1024
    row = pl.BlockSpec((tm, d), lambda i, f: (i, 0))
    return pl.pallas_call(
        _mlp_kernel,
        grid=(rows // tm, d_ff // tf),
        in_specs=[row, row,
                  pl.BlockSpec((d, tf), lambda i, f: (0, f)),
                  pl.BlockSpec((tf, d), lambda i, f: (f, 0)),
                  pl.BlockSpec((1, d), lambda i, f: (0, 0))],
        out_specs=row,
        out_shape=jax.ShapeDtypeStruct((rows, d), F32),
        scratch_shapes=[pltpu.VMEM((tm, d), F32)],
        compiler_params=_params("parallel", "arbitrary"),
        name="mlp",
    )(x1, hm, w_up, w_down, g_post_mlp)


def _project_all(h, w_segs, cos, sin, v_block=None):
    _, qa_t = _project(h, w_segs[0], "q_rope", cos, sin)
    if v_block is None:
        ka_f, ka_t = _project(h, w_segs[1], "k_rope", cos, sin)
    else:
        ka_f, ka_t = _project(h, w_segs[1], "k_rope_t", cos, sin, tk=cos.shape[0])
    if v_block is None:
        va_f, va_t = _project(h, w_segs[2], "kv")
    else:
        va_f, va_t = _project(h, w_segs[2], "v_blocks", tk=v_block)
    _, qb_t = _project(h, w_segs[3], "q")
    kb_f, kb_t = _project(h, w_segs[4], "kv")
    vb_f, vb_t = _project(h, w_segs[5], "kv")
    return (qa_t, ka_t, va_t, qb_t, kb_t, vb_t), (ka_f, va_f, kb_f, vb_f)


def _tail(x, h, oa, ob, w):
    merged = _merge(h, oa, ob, w["gate"], w["proj_a"], w["proj_b"])
    x1, hm = _out_proj(x, merged, w["out"], w["g_post_mix"], w["g_pre_mlp"])
    return _mlp(x1, hm, w["up"], w["down"], w["g_post_mlp"])


def kernel(x_prompt, x_sample, cache_diff_k, cache_diff_v, cache_sb_k, cache_sb_v, g_pre_mix, w_in, lambda_q1, lambda_k1, lambda_q2, lambda_k2, g_diff_head, w_gate, w_proj_a, w_proj_b, w_out, g_post_mix, g_pre_mlp, w_up, w_down, g_post_mlp):
    bp, tp, d = x_prompt.shape
    bs, ts, _ = x_sample.shape
    depth, _, past = cache_diff_k.shape[:3]
    assert depth == 1 and tp % CHUNK == 0

    w_in_b = w_in[0].astype(BF16)
    w_segs = [w_in_b[:, s * SEG_W:(s + 1) * SEG_W] for s in range(6)]
    w = dict(gate=w_gate[0].astype(BF16), proj_a=w_proj_a[0].astype(BF16), proj_b=w_proj_b[0].astype(BF16),
             out=w_out[0].astype(BF16), up=w_up[0].astype(BF16), down=w_down[0].astype(BF16),
             g_post_mix=g_post_mix, g_pre_mlp=g_pre_mlp, g_post_mlp=g_post_mlp)
    lam_vecs = jnp.concatenate([lambda_q1, lambda_k1, lambda_q2, lambda_k2], axis=0).astype(F32)

    xp = x_prompt.reshape(bp * tp, d)
    hp = _prenorm(xp, g_pre_mix)
    cos_p, sin_p = _rope_tables(jnp.arange(tp, dtype=jnp.int32))
    tk_diff = _tile(tp, 256, CHUNK)
    (qa_t, ka_t, va_tt, qb_t, kb_t, vb_t), kv_p = _project_all(hp, w_segs, cos_p, sin_p, v_block=tk_diff)
    oa = _diff_prompt(qa_t, ka_t, va_tt, lam_vecs, g_diff_head, bp, tp, tk_diff)
    ob = _stick_prompt(qb_t, kb_t, vb_t, bp, tp)
    yp = _tail(xp, hp, oa, ob, w).reshape(bp, tp, d)

    xs = x_sample.reshape(bs * ts, d)
    hs = _prenorm(xs, g_pre_mix)
    pos_s = past + jnp.arange(ts, dtype=jnp.int32)
    cos_s, sin_s = _rope_tables(jnp.tile(pos_s, bs))
    (qa_t, ka_t, va_t, qb_t, kb_t, vb_t), kv_s = _project_all(hs, w_segs, cos_s, sin_s)
    oa = _diff_sample(qa_t, ka_t, va_t, jnp.transpose(cache_diff_k[0], (0, 2, 3, 1)), cache_diff_v[0],
                      lam_vecs, g_diff_head, bs, ts, past)
    ob = _stick_sample(qb_t, kb_t, vb_t, cache_sb_k[0], cache_sb_v[0], bs, ts, past)
    ys = _tail(xs, hs, oa, ob, w).reshape(bs, ts, d)

    def caches(kv, b, t):
        ka_f, va_f, kb_f, vb_f = kv
        if ka_f.ndim == 4:
            ka_f = jnp.transpose(ka_f, (0, 3, 1, 2))
        return (ka_f.reshape(1, b, t, 2 * HA, HD_A), va_f.reshape(1, b, t, HA, VD_A),
                kb_f.reshape(1, b, t, HB, HD_B), vb_f.reshape(1, b, t, HB, HD_B))

    return (yp, ys) + caches(kv_p, bp, tp) + caches(kv_s, bs, ts)
```

```python
import functools
import math

import jax
import jax.numpy as jnp
from jax import lax
from jax.experimental import pallas as pl
from jax.experimental.pallas import tpu as pltpu

F32 = jnp.float32
BF16 = jnp.bfloat16

CHUNK = 64
HA = 8
HD_A = 64
VD_A = 128
HB = 8
HD_B = 128
SEG_W = 1024
LANES = 128
N_GROUPS = SEG_W // LANES
ROPE_THETA = 10000.0
EPS = 1e-6
NEG = -1e30
LAM_INIT = 0.8 - 0.6 * math.exp(-0.3 * 0)
UNDERFLOW_LOG = -104.0
DENOM_MIN = 1e-20
ONES_ROWS = 16
VMEM_LIMIT = 56 * 1024 * 1024


def _tile(n, pref, mult=8):
    if n <= pref:
        return n
    for t in range(pref, 0, -1):
        if n % t == 0 and t % mult == 0:
            return t
    raise ValueError(f"no tile for {n}")


def _params(*sem):
    return pltpu.CompilerParams(dimension_semantics=sem, vmem_limit_bytes=VMEM_LIMIT)


def _dot(a, b):
    return jnp.dot(a, b, preferred_element_type=F32)


def _dot_nt(a, b):
    return lax.dot_general(a, b, (((1,), (1,)), ((), ())), preferred_element_type=F32)


def _rms(x, g):
    return x * lax.rsqrt(jnp.mean(x * x, axis=-1, keepdims=True) + EPS) * g


def _prenorm_kernel(x_ref, g_ref, h_ref):
    h_ref[...] = _rms(x_ref[...], g_ref[...]).astype(BF16)


def _prenorm(x, g):
    rows, d = x.shape
    tm = _tile(rows, 512)
    return pl.pallas_call(
        _prenorm_kernel,
        grid=(rows // tm,),
        in_specs=[pl.BlockSpec((tm, d), lambda i: (i, 0)), pl.BlockSpec((1, d), lambda i: (0, 0))],
        out_specs=pl.BlockSpec((tm, d), lambda i: (i, 0)),
        out_shape=jax.ShapeDtypeStruct((rows, d), BF16),
        compiler_params=_params("parallel"),
        name="prenorm",
    )(x, g)


PROJ_CHUNK = 2


def _project_chunks(h_ref, w_ref):
    h = h_ref[...]
    for c0 in range(0, N_GROUPS, PROJ_CHUNK):
        yield c0, _dot(h, w_ref[:, c0 * LANES:(c0 + PROJ_CHUNK) * LANES])


def _groups(z):
    return [z[:, c * LANES:(c + 1) * LANES] for c in range(z.shape[1] // LANES)]


def _store_groups(t_ref, z, c0):
    for c, zc in enumerate(_groups(z)):
        t_ref[c0 + c] = zc.astype(BF16)


def _rope_groups(z, cos, sin):
    lane = lax.broadcasted_iota(jnp.int32, (1, LANES), 1)
    first_half = (lane % HD_A) < (HD_A // 2)
    out = []
    for zc in _groups(z):
        partner = jnp.where(first_half, pltpu.roll(zc, LANES - HD_A // 2, 1), pltpu.roll(zc, HD_A // 2, 1))
        out.append(zc * cos + partner * sin)
    return out


def _store_heads(f_ref, z, c0):
    for c, zc in enumerate(_groups(z)):
        f_ref[:, c0 + c, :] = zc


def _proj_kv_kernel(h_ref, w_ref, f_ref, t_ref):
    for c0, z in _project_chunks(h_ref, w_ref):
        _store_heads(f_ref, z, c0)
        _store_groups(t_ref, z, c0)


def _proj_v_blocks_kernel(h_ref, w_ref, f_ref, t_ref, *, tk):
    for c0, z in _project_chunks(h_ref, w_ref):
        _store_heads(f_ref, z, c0)
        for c, zc in enumerate(_groups(z)):
            zt = zc.T
            for j in range(zt.shape[1] // tk):
                t_ref[c0 + c, j, :LANES, :] = zt[:, j * tk:(j + 1) * tk].astype(BF16)
    row = lax.broadcasted_iota(jnp.int32, (ONES_ROWS, tk), 0)
    ones_rows = jnp.where(row == 0, 1.0, 0.0).astype(BF16)
    for c in range(N_GROUPS):
        for j in range(t_ref.shape[1]):
            t_ref[c, j, LANES:, :] = ones_rows


def _proj_q_kernel(h_ref, w_ref, t_ref):
    for c0, z in _project_chunks(h_ref, w_ref):
        _store_groups(t_ref, z, c0)


def _proj_k_rope_kernel(h_ref, w_ref, cos_ref, sin_ref, f_ref, t_ref):
    for c0, z in _project_chunks(h_ref, w_ref):
        for c, zc in enumerate(_rope_groups(z, cos_ref[...], sin_ref[...]), c0):
            f_ref[:, c * LANES:(c + 1) * LANES] = zc
            t_ref[c] = zc.astype(BF16)


def _proj_k_rope_t_kernel(h_ref, w_ref, cos_ref, sin_ref, f_ref, t_ref):
    for c0, z in _project_chunks(h_ref, w_ref):
        for c, zc in enumerate(_rope_groups(z, cos_ref[...], sin_ref[...]), c0):
            zt = zc.T
            f_ref[0, 2 * c] = zt[:HD_A]
            f_ref[0, 2 * c + 1] = zt[HD_A:]
            t_ref[c] = zc.astype(BF16)


def _proj_q_rope_kernel(h_ref, w_ref, cos_ref, sin_ref, t_ref):
    for c0, z in _project_chunks(h_ref, w_ref):
        for c, zc in enumerate(_rope_groups(z, cos_ref[...], sin_ref[...]), c0):
            t_ref[c] = (zc * (HD_A ** -0.5)).astype(BF16)


def _project(h, w, mode, cos=None, sin=None, tk=None):
    rows, d = h.shape
    tm = _tile(rows, 512)
    n_tab = None if cos is None else cos.shape[0] // tm
    in_specs = [pl.BlockSpec((tm, d), lambda i: (i, 0)), pl.BlockSpec((d, SEG_W), lambda i: (0, 0))]
    args = [h, w]
    if cos is not None:
        tab = pl.BlockSpec((tm, LANES), lambda i: (i % n_tab, 0))
        in_specs += [tab, tab]
        args += [cos, sin]
    t_spec = pl.BlockSpec((N_GROUPS, tm, LANES), lambda i: (0, i, 0))
    t_shape = jax.ShapeDtypeStruct((N_GROUPS, rows, LANES), BF16)
    with_f32 = mode in ("kv", "k_rope", "k_rope_t", "v_blocks")
    body = {"kv": _proj_kv_kernel, "q": _proj_q_kernel, "k_rope": _proj_k_rope_kernel,
            "k_rope_t": _proj_k_rope_t_kernel, "q_rope": _proj_q_rope_kernel,
            "v_blocks": functools.partial(_proj_v_blocks_kernel, tk=tk)}[mode]
    if mode == "k_rope":
        f_spec = pl.BlockSpec((tm, SEG_W), lambda i: (i, 0))
        f_shape = jax.ShapeDtypeStruct((rows, SEG_W), F32)
    elif mode == "k_rope_t":
        f_spec = pl.BlockSpec((1, 2 * HA, HD_A, tm), lambda i: (i // n_tab, 0, 0, i % n_tab))
        f_shape = jax.ShapeDtypeStruct((rows // tk, 2 * HA, HD_A, tk), F32)
    else:
        f_spec = pl.BlockSpec((tm, N_GROUPS, LANES), lambda i: (i, 0, 0))
        f_shape = jax.ShapeDtypeStruct((rows, N_GROUPS, LANES), F32)
    if mode == "v_blocks":
        t_spec = pl.BlockSpec((N_GROUPS, tm // tk, LANES + ONES_ROWS, tk), lambda i: (0, i, 0, 0))
        t_shape = jax.ShapeDtypeStruct((N_GROUPS, rows // tk, LANES + ONES_ROWS, tk), BF16)
    out = pl.pallas_call(
        body,
        grid=(rows // tm,),
        in_specs=in_specs,
        out_specs=[f_spec, t_spec] if with_f32 else t_spec,
        out_shape=[f_shape, t_shape] if with_f32 else t_shape,
        compiler_params=_params("parallel"),
        name="proj_" + mode,
    )(*args)
    return (out[0], out[1]) if with_f32 else (None, out)


def _rope_tables(pos):
    half = HD_A // 2
    inv = ROPE_THETA ** (-jnp.arange(half, dtype=F32) / half)
    ang = pos.astype(F32)[:, None] * inv[None, :]
    cos, sin = jnp.cos(ang), jnp.sin(ang)
    reps = LANES // HD_A
    return (jnp.tile(jnp.concatenate([cos, cos], axis=-1), (1, reps)),
            jnp.tile(jnp.concatenate([-sin, sin], axis=-1), (1, reps)))


def _head_lanes(q, c):
    lane = lax.broadcasted_iota(jnp.int32, (1, LANES), 1)
    return jnp.where((lane >= c * HD_A) & (lane < (c + 1) * HD_A), q, jnp.zeros_like(q))


def _suffix_matrix(n):
    j = lax.broadcasted_iota(jnp.int32, (2 * n, n), 0) % n
    s = lax.broadcasted_iota(jnp.int32, (2 * n, n), 1)
    return jnp.where(j >= s, 1.0, 0.0).astype(BF16)


def _log_keep_terms(z, mask):
    neg = -z
    log_keep = jnp.minimum(neg, 0.0) - jnp.log(1.0 + jnp.exp(jnp.minimum(z, neg)))
    if mask is not None:
        log_keep = jnp.where(mask, log_keep, 0.0)
    hi = log_keep.astype(BF16)
    return jnp.concatenate([hi, (log_keep - hi.astype(F32)).astype(BF16)], axis=1)


def _stick_block(z, mask, carry, v, acc_ref, suffix_mat):
    suffix = _dot(_log_keep_terms(z, mask), suffix_mat)
    a = jnp.exp(z + suffix + carry)
    if mask is not None:
        a = jnp.where(mask, a, 0.0)
    acc_ref[...] += _dot(a.astype(BF16), v)
    return carry + suffix[:, 0:1]


def _lambda(lam_ref):
    lam = lam_ref[...]
    d1 = jnp.sum(lam[0:1] * lam[1:2], axis=-1, keepdims=True)
    d2 = jnp.sum(lam[2:3] * lam[3:4], axis=-1, keepdims=True)
    return jnp.exp(d1) - jnp.exp(d2) + LAM_INIT


def _diff_finish(o1, o2, lam, g_head):
    o = o1 - lam * o2
    o = o * lax.rsqrt(jnp.mean(o * o, axis=-1, keepdims=True) + EPS)
    return (o * g_head * (1.0 - LAM_INIT)).astype(BF16)


def _key_norm_bound(k_ref, c, seq, tq):
    rows = _tile(seq, 2048, 16)
    lane = lax.broadcasted_iota(jnp.int32, (8, LANES), 1)
    ones_c = jnp.where((lane >= c * HD_A) & (lane < (c + 1) * HD_A), 1.0, 0.0).astype(BF16)

    def body(i, best):
        k = k_ref[0, pl.ds(pl.multiple_of(i * rows, rows), rows), :].astype(F32)
        return jnp.maximum(best, _dot_nt(ones_c, (k * k).astype(BF16)))

    best = lax.fori_loop(0, seq // rows, body, jnp.zeros((8, rows), F32))
    return jnp.broadcast_to(jnp.max(best[0:1], axis=1, keepdims=True), (1, tq))


def _diff_prompt_kernel(q1_ref, q2_ref, k1_ref, k2_ref, vt_ref, lam_ref, g_ref, o_ref, acc_ref, s_ref, p_ref,
                        kmax_ref, *, tq, tk, seq):
    qi = pl.program_id(2)
    lam = _lambda(lam_ref)

    @pl.when(qi == 0)
    def _():
        for idx, (c, k_ref) in enumerate((c, k_ref) for c in range(2) for k_ref in (k1_ref, k2_ref)):
            kmax_ref[idx] = _key_norm_bound(k_ref, c, seq, tq)

    n_full = (qi * tq) // tk
    n_edge = tq // tk
    q_pos = qi * tq + lax.broadcasted_iota(jnp.int32, (1, tq), 1)
    limit = (q_pos // CHUNK + 1) * CHUNK
    k_pos = n_full * tk + lax.broadcasted_iota(jnp.int32, (tk, 1), 0)
    edge_masks = [k_pos + e * tk < limit for e in range(n_edge)]

    streams = [(c, _head_lanes(q_ref[0], c), k_ref)
               for c in range(2) for q_ref, k_ref in ((q1_ref, k1_ref), (q2_ref, k2_ref))]
    n_streams = len(streams)

    def score_block(j):
        start = pl.multiple_of(j * tk, tk)
        return [_dot_nt(k_ref[0, pl.ds(start, tk), :], q) for _, q, k_ref in streams]

    def after(x):
        return 0.0 * x

    def store_scores(scores, readers=None):
        for idx, s in enumerate(scores):
            s_ref[idx] = s if readers is None else s + readers[idx]

    def value_block(j, alphas, slot=1):
        done = []
        for idx, (c, _, _) in enumerate(streams):
            pv = _dot(vt_ref[c, j], p_ref[slot, idx])
            acc_ref[idx] = pv + (acc_ref[idx] if alphas is None else alphas[idx] * acc_ref[idx])
            done.append(after(pv[VD_A:VD_A + 1, :]))
        return done

    def masked(s, mask):
        return s if mask is None else jnp.where(mask, s, NEG)

    ones8 = jnp.ones((8, LANES), BF16)
    bounds = []
    for idx, (_, q, _) in enumerate(streams):
        qf = q.astype(F32)
        q_sq = _dot_nt(ones8, (qf * qf).astype(BF16))[0:1]
        bounds.append(jnp.sqrt(q_sq * kmax_ref[idx]) * 1.02)

    def store_weights(scores, mask, slot, readers=None):
        for idx, s in enumerate(scores):
            shift = bounds[idx] if readers is None else bounds[idx] + readers[idx]
            p_ref[slot, idx] = jnp.exp(masked(s, mask) - shift).astype(BF16)

    def bound_block(j, mask):
        scores = score_block(j)
        done = value_block(jnp.maximum(j - 1, 0), None)
        store_weights(scores, mask, 1, done)

    def bound_pair(i, _):
        j = 2 * i
        scores_a = score_block(j)
        done_prev = value_block(jnp.maximum(j - 1, 0), None)
        scores_b = score_block(j + 1)
        store_weights(scores_a, None, 0)
        value_block(j, None, slot=0)
        store_weights(scores_b, None, 1, done_prev)

    def bound_quad(i, _):
        j = 4 * i
        scores = score_block(j)
        done = {1: value_block(jnp.maximum(j - 1, 0), None), 0: None}
        for u in range(4):
            slot = u % 2
            next_scores = score_block(j + u + 1) if u < 3 else None
            store_weights(scores, None, slot, done[slot])
            if u < 3:
                done[slot] = value_block(j + u, None, slot=slot)
            scores = next_scores

    acc_ref[...] = jnp.zeros_like(acc_ref)
    p_ref[...] = jnp.zeros_like(p_ref)
    quads = n_full // 4
    lax.fori_loop(0, quads, bound_quad, None)
    lax.fori_loop(2 * quads, n_full // 2, bound_pair, None)

    @pl.when(n_full % 2 == 1)
    def _():
        bound_block(n_full - 1, None)

    for e, mask in enumerate(edge_masks):
        bound_block(n_full + e, mask)
    value_block(n_full + n_edge - 1, None)

    denom_min = acc_ref[0, VD_A:VD_A + 1, :]
    for idx in range(1, n_streams):
        denom_min = jnp.minimum(denom_min, acc_ref[idx, VD_A:VD_A + 1, :])

    @pl.when(jnp.logical_not(jnp.min(denom_min) >= DENOM_MIN))
    def _():
        def max_block(maxima, mask, done):
            new_maxima, alphas = [], []
            for idx in range(n_streams):
                s = masked(s_ref[idx], mask)
                m_new = jnp.maximum(maxima[idx], jnp.max(s, axis=0, keepdims=True))
                alphas.append(jnp.exp(maxima[idx] - m_new))
                new_maxima.append(m_new)
                p_ref[1, idx] = jnp.exp(s - (m_new + done[idx])).astype(BF16)
            return tuple(new_maxima), tuple(alphas)

        def step(j, carry):
            maxima, alphas = carry
            next_scores = score_block(j + 1)
            done = value_block(jnp.maximum(j - 1, 0), alphas)
            maxima, alphas = max_block(maxima, None, done)
            store_scores(next_scores, [after(m) for m in maxima])
            return maxima, alphas

        acc_ref[...] = jnp.zeros_like(acc_ref)
        p_ref[...] = jnp.zeros_like(p_ref)
        store_scores(score_block(0))
        maxima = tuple(jnp.full((1, tq), NEG, F32) for _ in streams)
        alphas = tuple(jnp.ones((1, tq), F32) for _ in streams)
        maxima, alphas = lax.fori_loop(0, n_full, step, (maxima, alphas))
        done = value_block(jnp.maximum(n_full - 1, 0), alphas)
        for e, mask in enumerate(edge_masks):
            if e > 0:
                store_scores(score_block(n_full + e), [after(m) for m in maxima])
            maxima, alphas = max_block(maxima, mask, done)
            done = value_block(n_full + e, alphas)

    for c in range(2):
        a1, a2 = acc_ref[2 * c], acc_ref[2 * c + 1]
        o = a1[:VD_A] / a1[VD_A:VD_A + 1] - lam * (a2[:VD_A] / a2[VD_A:VD_A + 1])
        o = o * lax.rsqrt(jnp.mean(o * o, axis=0, keepdims=True) + EPS)
        o = o * g_ref[...] * (1.0 - LAM_INIT)
        o_ref[:, c * VD_A:(c + 1) * VD_A] = o.T.astype(BF16)


def _diff_prompt(qa_t, ka_t, va_tt, lam_vecs, g_head, batch, seq, tk):
    tq = 2 * tk if seq % (2 * tk) == 0 else tk
    nq = seq // tq
    half = N_GROUPS // 2
    q_spec = lambda off: pl.BlockSpec((1, tq, LANES), lambda b, p, i: (p + off, b * nq + i, 0))
    k_spec = lambda off: pl.BlockSpec((1, seq, LANES), lambda b, p, i: (p + off, b, 0))
    return pl.pallas_call(
        functools.partial(_diff_prompt_kernel, tq=tq, tk=tk, seq=seq),
        grid=(batch, half, nq),
        in_specs=[q_spec(0), q_spec(half), k_spec(0), k_spec(half),
                  pl.BlockSpec((2, seq // tk, VD_A + ONES_ROWS, tk), lambda b, p, i: (p, b, 0, 0)),
                  pl.BlockSpec((4, HD_A), lambda b, p, i: (0, 0)),
                  pl.BlockSpec((VD_A, 1), lambda b, p, i: (0, 0))],
        out_specs=pl.BlockSpec((tq, 2 * VD_A), lambda b, p, i: (b * nq + i, p)),
        out_shape=jax.ShapeDtypeStruct((batch * seq, SEG_W), BF16),
        scratch_shapes=[pltpu.VMEM((4, VD_A + ONES_ROWS, tq), F32), pltpu.VMEM((4, tk, tq), F32),
                        pltpu.VMEM((2, 4, tk, tq), BF16), pltpu.VMEM((4, 1, tq), F32)],
        compiler_params=_params("arbitrary", "arbitrary", "arbitrary"),
        name="diff_prompt",
    )(qa_t, qa_t, ka_t, ka_t, va_tt, lam_vecs, g_head.reshape(VD_A, 1))


def _stick_blocks(zs, mask, carries, vs, acc_ref, suffix_mat):
    terms = [_log_keep_terms(z, mask) for z in zs]
    suffixes = [_dot(t, suffix_mat) for t in terms]
    weights = []
    for z, suffix, carry in zip(zs, suffixes, carries):
        a = jnp.exp(z + suffix + carry)
        if mask is not None:
            a = jnp.where(mask, a, 0.0)
        weights.append(a.astype(BF16))
    for idx, (a, v) in enumerate(zip(weights, vs)):
        acc_ref[idx] += _dot(a, v)
    return tuple(carry + suffix[:, 0:1] for carry, suffix in zip(carries, suffixes))


def _stick_prompt_kernel(q_ref, k_ref, v_ref, o_ref, acc_ref, *, tq, tk, heads):
    qi = pl.program_id(2)
    qs = [q_ref[h] for h in range(heads)]
    suffix_mat = _suffix_matrix(tk)
    scale = HD_B ** -0.5
    acc_ref[...] = jnp.zeros_like(acc_ref)
    q_pos = qi * tq + lax.broadcasted_iota(jnp.int32, (tq, 1), 0)

    def block(start, mask, carries):
        zs = [_dot_nt(qs[h], k_ref[h, pl.ds(start, tk), :]) * scale for h in range(heads)]
        vs = [v_ref[h, pl.ds(start, tk), :] for h in range(heads)]
        return _stick_blocks(zs, mask, carries, vs, acc_ref, suffix_mat)

    carries = tuple(jnp.zeros((tq, 1), F32) for _ in range(heads))
    for e in range(tq // tk - 1, -1, -1):
        start = pl.multiple_of(qi * tq + e * tk, tk)
        k_pos = start + lax.broadcasted_iota(jnp.int32, (1, tk), 1)
        carries = block(start, k_pos < q_pos, carries)

    n_full = (qi * tq) // tk

    def cond(st):
        i, carries = st
        largest = functools.reduce(jnp.maximum, [jnp.max(c) for c in carries])
        return jnp.logical_and(i < n_full, largest > UNDERFLOW_LOG)

    def body(st):
        i, carries = st
        return i + 1, block(pl.multiple_of((n_full - 1 - i) * tk, tk), None, carries)

    lax.while_loop(cond, body, (jnp.int32(0), carries))
    for h in range(heads):
        o_ref[:, h * HD_B:(h + 1) * HD_B] = acc_ref[h].astype(BF16)


def _stick_prompt(qb_t, kb_t, vb_t, batch, seq):
    tq = _tile(seq, 256, 8)
    tk = tq
    heads = 4
    nq = seq // tq
    kv_spec = pl.BlockSpec((heads, seq, LANES), lambda b, h, i: (h, b, 0), pipeline_mode=pl.Buffered(1))
    return pl.pallas_call(
        functools.partial(_stick_prompt_kernel, tq=tq, tk=tk, heads=heads),
        grid=(batch, HB // heads, nq),
        in_specs=[pl.BlockSpec((heads, tq, LANES), lambda b, h, i: (h, b * nq + i, 0)), kv_spec, kv_spec],
        out_specs=pl.BlockSpec((tq, heads * HD_B), lambda b, h, i: (b * nq + i, h)),
        out_shape=jax.ShapeDtypeStruct((batch * seq, SEG_W), BF16),
        scratch_shapes=[pltpu.VMEM((heads, tq, HD_B), F32)],
        compiler_params=_params("parallel", "parallel", "arbitrary"),
        name="stick_prompt",
    )(qb_t, kb_t, vb_t)


def _diff_sample_kernel(qa_ref, ka_ref, va_ref, kc_ref, vc_ref, lam_ref, g_ref, o_ref, *, ts, past):
    lam = _lambda(lam_ref)
    q_pos = past + lax.broadcasted_iota(jnp.int32, (ts, 1), 0)
    limit = (q_pos // CHUNK + 1) * CHUNK
    k_pos = past + lax.broadcasted_iota(jnp.int32, (1, ts), 1)
    new_mask = k_pos < limit

    def attend(g, c, v_cache, v_new):
        q = _head_lanes(qa_ref[g], c)
        k_cache = jnp.concatenate([kc_ref[0, 2 * g].astype(BF16), kc_ref[0, 2 * g + 1].astype(BF16)], axis=0)
        s_cache = _dot(q, k_cache)
        s_new = jnp.where(new_mask, _dot_nt(q, ka_ref[g]), NEG)
        m = jnp.maximum(jnp.max(s_cache, axis=-1, keepdims=True), jnp.max(s_new, axis=-1, keepdims=True))
        p_cache = jnp.exp(s_cache - m)
        p_new = jnp.exp(s_new - m)
        l = jnp.sum(p_cache, axis=-1, keepdims=True) + jnp.sum(p_new, axis=-1, keepdims=True)
        return (_dot(p_cache.astype(BF16), v_cache) + _dot(p_new.astype(BF16), v_new)) / l

    half = N_GROUPS // 2
    for h in range(HA):
        g, c = h // 2, h % 2
        v_cache = vc_ref[0, :, h, :].astype(BF16)
        o1 = attend(g, c, v_cache, va_ref[h])
        o2 = attend(g + half, c, v_cache, va_ref[h])
        o_ref[:, h * VD_A:(h + 1) * VD_A] = _diff_finish(o1, o2, lam, g_ref[...])


def _diff_sample(qa_t, ka_t, va_t, cache_k, cache_v, lam_vecs, g_head, batch, ts, past):
    new_spec = pl.BlockSpec((N_GROUPS, ts, LANES), lambda b: (0, b, 0))
    return pl.pallas_call(
        functools.partial(_diff_sample_kernel, ts=ts, past=past),
        grid=(batch,),
        in_specs=[new_spec, new_spec, new_spec,
                  pl.BlockSpec((1, 2 * HA, HD_A, past), lambda b: (b, 0, 0, 0)),
                  pl.BlockSpec((1, past, HA, VD_A), lambda b: (b, 0, 0, 0)),
                  pl.BlockSpec((4, HD_A), lambda b: (0, 0)),
                  pl.BlockSpec((1, VD_A), lambda b: (0, 0))],
        out_specs=pl.BlockSpec((ts, SEG_W), lambda b: (b, 0)),
        out_shape=jax.ShapeDtypeStruct((batch * ts, SEG_W), BF16),
        compiler_params=_params("parallel"),
        name="diff_sample",
    )(qa_t, ka_t, va_t, cache_k, cache_v, lam_vecs, g_head)


def _stick_sample_kernel(q_ref, kn_ref, vn_ref, kc_ref, vc_ref, o_ref, acc_ref, *, ts, past, tk):
    scale = HD_B ** -0.5
    row = lax.broadcasted_iota(jnp.int32, (ts, 1), 0)
    col = lax.broadcasted_iota(jnp.int32, (1, ts), 1)
    new_suffix_mat = _suffix_matrix(ts)
    suffix_mat = _suffix_matrix(tk)
    n_full = past // tk

    for h in range(HB):
        q = q_ref[h]
        acc_ref[...] = jnp.zeros_like(acc_ref)
        z = _dot_nt(q, kn_ref[h]) * scale
        carry = _stick_block(z, col < row, jnp.zeros((ts, 1), F32), vn_ref[h], acc_ref, new_suffix_mat)

        def cond(st):
            i, carry = st
            return jnp.logical_and(i < n_full, jnp.max(carry) > UNDERFLOW_LOG)

        def body(st, q=q, h=h):
            i, carry = st
            start = pl.multiple_of((n_full - 1 - i) * tk, tk)
            k = kc_ref[0, pl.ds(start, tk), h, :].astype(BF16)
            v = vc_ref[0, pl.ds(start, tk), h, :].astype(BF16)
            return i + 1, _stick_block(_dot_nt(q, k) * scale, None, carry, v, acc_ref, suffix_mat)

        lax.while_loop(cond, body, (jnp.int32(0), carry))
        o_ref[:, h * HD_B:(h + 1) * HD_B] = acc_ref[...].astype(BF16)


def _stick_sample(qb_t, kb_t, vb_t, cache_k, cache_v, batch, ts, past):
    tk = _tile(past, 256, 8)
    new_spec = pl.BlockSpec((N_GROUPS, ts, LANES), lambda b: (0, b, 0))
    cache_spec = pl.BlockSpec((1, past, HB, HD_B), lambda b: (b, 0, 0, 0))
    return pl.pallas_call(
        functools.partial(_stick_sample_kernel, ts=ts, past=past, tk=tk),
        grid=(batch,),
        in_specs=[new_spec, new_spec, new_spec, cache_spec, cache_spec],
        out_specs=pl.BlockSpec((ts, SEG_W), lambda b: (b, 0)),
        out_shape=jax.ShapeDtypeStruct((batch * ts, SEG_W), BF16),
        scratch_shapes=[pltpu.VMEM((ts, HD_B), F32)],
        compiler_params=_params("parallel"),
        name="stick_sample",
    )(qb_t, kb_t, vb_t, cache_k, cache_v)


def _merge_kernel(h_ref, oa_ref, ob_ref, wga_ref, wgb_ref, wpa_ref, wpb_ref, o_ref):
    h = h_ref[...]
    ga = jax.nn.sigmoid(_dot(h, wga_ref[...]))
    gb = jax.nn.sigmoid(_dot(h, wgb_ref[...]))
    ya = _dot(oa_ref[...], wpa_ref[...])
    yb = _dot(ob_ref[...], wpb_ref[...])
    o_ref[...] = (ga * ya + gb * yb).astype(BF16)


def _merge(h, oa, ob, w_gate, w_proj_a, w_proj_b):
    rows, d = h.shape
    tm = _tile(rows, 512)
    tn = 512
    nn = d // tn
    row_spec = lambda w: pl.BlockSpec((tm, w), lambda i, n: (i, 0))
    return pl.pallas_call(
        _merge_kernel,
        grid=(rows // tm, nn),
        in_specs=[row_spec(d), row_spec(SEG_W), row_spec(SEG_W),
                  pl.BlockSpec((d, tn), lambda i, n: (0, n)),
                  pl.BlockSpec((d, tn), lambda i, n: (0, nn + n)),
                  pl.BlockSpec((SEG_W, tn), lambda i, n: (0, n)),
                  pl.BlockSpec((SEG_W, tn), lambda i, n: (0, n))],
        out_specs=pl.BlockSpec((tm, tn), lambda i, n: (i, n)),
        out_shape=jax.ShapeDtypeStruct((rows, d), BF16),
        compiler_params=_params("parallel", "arbitrary"),
        name="merge",
    )(h, oa, ob, w_gate, w_gate, w_proj_a, w_proj_b)


def _out_kernel(x_ref, m_ref, w_ref, g_post_ref, g_pre_ref, x1_ref, hm_ref):
    x1 = x_ref[...] + _rms(_dot(m_ref[...], w_ref[...]), g_post_ref[...])
    x1_ref[...] = x1
    hm_ref[...] = _rms(x1, g_pre_ref[...]).astype(BF16)


def _out_proj(x, merged, w_out, g_post_mix, g_pre_mlp):
    rows, d = x.shape
    tm = _tile(rows, 256)
    row = pl.BlockSpec((tm, d), lambda i: (i, 0))
    vec = pl.BlockSpec((1, d), lambda i: (0, 0))
    return pl.pallas_call(
        _out_kernel,
        grid=(rows // tm,),
        in_specs=[row, row, pl.BlockSpec((d, d), lambda i: (0, 0)), vec, vec],
        out_specs=[row, row],
        out_shape=[jax.ShapeDtypeStruct((rows, d), F32), jax.ShapeDtypeStruct((rows, d), BF16)],
        compiler_params=_params("parallel"),
        name="out_proj",
    )(x, merged, w_out, g_post_mix, g_pre_mlp)


def _mlp_kernel(x1_ref, hm_ref, wu_ref, wd_ref, g_ref, y_ref, acc_ref):
    f = pl.program_id(1)

    @pl.when(f == 0)
    def _():
        acc_ref[...] = jnp.zeros_like(acc_ref)

    hm = hm_ref[...]
    half = wu_ref.shape[1] // 2
    down = []
    for s in range(2):
        u = jnp.square(jnp.maximum(_dot(hm, wu_ref[:, s * half:(s + 1) * half]), 0.0))
        down.append(_dot(u.astype(BF16), wd_ref[s * half:(s + 1) * half, :]))
    acc_ref[...] += down[0] + down[1]

    @pl.when(f == pl.num_programs(1) - 1)
    def _():
        y_ref[...] = x1_ref[...] + _rms(acc_ref[...], g_ref[...])


def _mlp(x1, hm, w_up, w_down, g_post_mlp):
    rows, d = x1.shape
    d_ff = w_up.shape[1]
    tm = _tile(rows, 512)
    tf = 1024
    row = pl.BlockSpec((tm, d), lambda i, f: (i, 0))
    return pl.pallas_call(
        _mlp_kernel,
        grid=(rows // tm, d_ff // tf),
        in_specs=[row, row,
                  pl.BlockSpec((d, tf), lambda i, f: (0, f)),
                  pl.BlockSpec((tf, d), lambda i, f: (f, 0)),
                  pl.BlockSpec((1, d), lambda i, f: (0, 0))],
        out_specs=row,
        out_shape=jax.ShapeDtypeStruct((rows, d), F32),
        scratch_shapes=[pltpu.VMEM((tm, d), F32)],
        compiler_params=_params("parallel", "arbitrary"),
        name="mlp",
    )(x1, hm, w_up, w_down, g_post_mlp)


def _project_all(h, w_segs, cos, sin, v_block=None):
    _, qa_t = _project(h, w_segs[0], "q_rope", cos, sin)
    if v_block is None:
        ka_f, ka_t = _project(h, w_segs[1], "k_rope", cos, sin)
    else:
        ka_f, ka_t = _project(h, w_segs[1], "k_rope_t", cos, sin, tk=cos.shape[0])
    if v_block is None:
        va_f, va_t = _project(h, w_segs[2], "kv")
    else:
        va_f, va_t = _project(h, w_segs[2], "v_blocks", tk=v_block)
    _, qb_t = _project(h, w_segs[3], "q")
    kb_f, kb_t = _project(h, w_segs[4], "kv")
    vb_f, vb_t = _project(h, w_segs[5], "kv")
    return (qa_t, ka_t, va_t, qb_t, kb_t, vb_t), (ka_f, va_f, kb_f, vb_f)


def _tail(x, h, oa, ob, w):
    merged = _merge(h, oa, ob, w["gate"], w["proj_a"], w["proj_b"])
    x1, hm = _out_proj(x, merged, w["out"], w["g_post_mix"], w["g_pre_mlp"])
    return _mlp(x1, hm, w["up"], w["down"], w["g_post_mlp"])


def kernel(x_prompt, x_sample, cache_diff_k, cache_diff_v, cache_sb_k, cache_sb_v, g_pre_mix, w_in, lambda_q1, lambda_k1, lambda_q2, lambda_k2, g_diff_head, w_gate, w_proj_a, w_proj_b, w_out, g_post_mix, g_pre_mlp, w_up, w_down, g_post_mlp):
    bp, tp, d = x_prompt.shape
    bs, ts, _ = x_sample.shape
    depth, _, past = cache_diff_k.shape[:3]
    assert depth == 1 and tp % CHUNK == 0

    w_in_b = w_in[0].astype(BF16)
    w_segs = [w_in_b[:, s * SEG_W:(s + 1) * SEG_W] for s in range(6)]
    w = dict(gate=w_gate[0].astype(BF16), proj_a=w_proj_a[0].astype(BF16), proj_b=w_proj_b[0].astype(BF16),
             out=w_out[0].astype(BF16), up=w_up[0].astype(BF16), down=w_down[0].astype(BF16),
             g_post_mix=g_post_mix, g_pre_mlp=g_pre_mlp, g_post_mlp=g_post_mlp)
    lam_vecs = jnp.concatenate([lambda_q1, lambda_k1, lambda_q2, lambda_k2], axis=0).astype(F32)

    xp = x_prompt.reshape(bp * tp, d)
    hp = _prenorm(xp, g_pre_mix)
    cos_p, sin_p = _rope_tables(jnp.arange(tp, dtype=jnp.int32))
    tk_diff = _tile(tp, 256, CHUNK)
    (qa_t, ka_t, va_tt, qb_t, kb_t, vb_t), kv_p = _project_all(hp, w_segs, cos_p, sin_p, v_block=tk_diff)
    oa = _diff_prompt(qa_t, ka_t, va_tt, lam_vecs, g_diff_head, bp, tp, tk_diff)
    ob = _stick_prompt(qb_t, kb_t, vb_t, bp, tp)
    yp = _tail(xp, hp, oa, ob, w).reshape(bp, tp, d)

    xs = x_sample.reshape(bs * ts, d)
    hs = _prenorm(xs, g_pre_mix)
    pos_s = past + jnp.arange(ts, dtype=jnp.int32)
    cos_s, sin_s = _rope_tables(jnp.tile(pos_s, bs))
    (qa_t, ka_t, va_t, qb_t, kb_t, vb_t), kv_s = _project_all(hs, w_segs, cos_s, sin_s)
    oa = _diff_sample(qa_t, ka_t, va_t, jnp.transpose(cache_diff_k[0], (0, 2, 3, 1)), cache_diff_v[0],
                      lam_vecs, g_diff_head, bs, ts, past)
    ob = _stick_sample(qb_t, kb_t, vb_t, cache_sb_k[0], cache_sb_v[0], bs, ts, past)
    ys = _tail(xs, hs, oa, ob, w).reshape(bs, ts, d)

    def caches(kv, b, t):
        ka_f, va_f, kb_f, vb_f = kv
        if ka_f.ndim == 4:
            ka_f = jnp.transpose(ka_f, (0, 3, 1, 2))
        return (ka_f.reshape(1, b, t, 2 * HA, HD_A), va_f.reshape(1, b, t, HA, VD_A),
                kb_f.reshape(1, b, t, HB, HD_B), vb_f.reshape(1, b, t, HB, HD_B))

    return (yp, ys) + caches(kv_p, bp, tp) + caches(kv_s, bs, ts)
```

```python
import functools
import math

import jax
import jax.numpy as jnp
from jax import lax
from jax.experimental import pallas as pl
from jax.experimental.pallas import tpu as pltpu

F32 = jnp.float32
BF16 = jnp.bfloat16

CHUNK = 64
HA = 8
HD_A = 64
VD_A = 128
HB = 8
HD_B = 128
SEG_W = 1024
LANES = 128
N_GROUPS = SEG_W // LANES
ROPE_THETA = 10000.0
EPS = 1e-6
NEG = -1e30
LAM_INIT = 0.8 - 0.6 * math.exp(-0.3 * 0)
UNDERFLOW_LOG = -104.0
DENOM_MIN = 1e-20
TRIP_BLOCKS = (8, 4, 2)
ONES_ROWS = 16
VMEM_LIMIT = 56 * 1024 * 1024


def _tile(n, pref, mult=8):
    if n <= pref:
        return n
    for t in range(pref, 0, -1):
        if n % t == 0 and t % mult == 0:
            return t
    raise ValueError(f"no tile for {n}")


def _params(*sem):
    return pltpu.CompilerParams(dimension_semantics=sem, vmem_limit_bytes=VMEM_LIMIT)


def _dot(a, b):
    return jnp.dot(a, b, preferred_element_type=F32)


def _dot_nt(a, b):
    return lax.dot_general(a, b, (((1,), (1,)), ((), ())), preferred_element_type=F32)


def _rms(x, g):
    return x * lax.rsqrt(jnp.mean(x * x, axis=-1, keepdims=True) + EPS) * g


def _prenorm_kernel(x_ref, g_ref, h_ref):
    h_ref[...] = _rms(x_ref[...], g_ref[...]).astype(BF16)


def _prenorm(x, g):
    rows, d = x.shape
    tm = _tile(rows, 512)
    return pl.pallas_call(
        _prenorm_kernel,
        grid=(rows // tm,),
        in_specs=[pl.BlockSpec((tm, d), lambda i: (i, 0)), pl.BlockSpec((1, d), lambda i: (0, 0))],
        out_specs=pl.BlockSpec((tm, d), lambda i: (i, 0)),
        out_shape=jax.ShapeDtypeStruct((rows, d), BF16),
        compiler_params=_params("parallel"),
        name="prenorm",
    )(x, g)


PROJ_CHUNK = 2


def _project_chunks(h_ref, w_ref):
    h = h_ref[...]
    for c0 in range(0, N_GROUPS, PROJ_CHUNK):
        yield c0, _dot(h, w_ref[:, c0 * LANES:(c0 + PROJ_CHUNK) * LANES])


def _groups(z):
    return [z[:, c * LANES:(c + 1) * LANES] for c in range(z.shape[1] // LANES)]


def _store_groups(t_ref, z, c0):
    for c, zc in enumerate(_groups(z)):
        t_ref[c0 + c] = zc.astype(BF16)


def _rope_groups(z, cos, sin):
    lane = lax.broadcasted_iota(jnp.int32, (1, LANES), 1)
    first_half = (lane % HD_A) < (HD_A // 2)
    out = []
    for zc in _groups(z):
        partner = jnp.where(first_half, pltpu.roll(zc, LANES - HD_A // 2, 1), pltpu.roll(zc, HD_A // 2, 1))
        out.append(zc * cos + partner * sin)
    return out


def _store_heads(f_ref, z, c0):
    for c, zc in enumerate(_groups(z)):
        f_ref[:, c0 + c, :] = zc


def _proj_kv_kernel(h_ref, w_ref, f_ref, t_ref):
    for c0, z in _project_chunks(h_ref, w_ref):
        _store_heads(f_ref, z, c0)
        _store_groups(t_ref, z, c0)


def _proj_v_blocks_kernel(h_ref, w_ref, f_ref, t_ref, *, tk):
    for c0, z in _project_chunks(h_ref, w_ref):
        _store_heads(f_ref, z, c0)
        for c, zc in enumerate(_groups(z)):
            zt = zc.T
            for j in range(zt.shape[1] // tk):
                t_ref[c0 + c, j, :LANES, :] = zt[:, j * tk:(j + 1) * tk].astype(BF16)
    row = lax.broadcasted_iota(jnp.int32, (ONES_ROWS, tk), 0)
    ones_rows = jnp.where(row == 0, 1.0, 0.0).astype(BF16)
    for c in range(N_GROUPS):
        for j in range(t_ref.shape[1]):
            t_ref[c, j, LANES:, :] = ones_rows


def _proj_q_kernel(h_ref, w_ref, t_ref):
    for c0, z in _project_chunks(h_ref, w_ref):
        _store_groups(t_ref, z, c0)


def _proj_k_rope_kernel(h_ref, w_ref, cos_ref, sin_ref, f_ref, t_ref):
    for c0, z in _project_chunks(h_ref, w_ref):
        for c, zc in enumerate(_rope_groups(z, cos_ref[...], sin_ref[...]), c0):
            f_ref[:, c * LANES:(c + 1) * LANES] = zc
            t_ref[c] = zc.astype(BF16)


def _proj_k_rope_t_kernel(h_ref, w_ref, cos_ref, sin_ref, f_ref, t_ref):
    for c0, z in _project_chunks(h_ref, w_ref):
        for c, zc in enumerate(_rope_groups(z, cos_ref[...], sin_ref[...]), c0):
            zt = zc.T
            f_ref[0, 2 * c] = zt[:HD_A]
            f_ref[0, 2 * c + 1] = zt[HD_A:]
            t_ref[c] = zc.astype(BF16)


def _proj_q_rope_kernel(h_ref, w_ref, cos_ref, sin_ref, t_ref):
    for c0, z in _project_chunks(h_ref, w_ref):
        for c, zc in enumerate(_rope_groups(z, cos_ref[...], sin_ref[...]), c0):
            t_ref[c] = (zc * (HD_A ** -0.5)).astype(BF16)


def _project(h, w, mode, cos=None, sin=None, tk=None):
    rows, d = h.shape
    tm = _tile(rows, 512)
    n_tab = None if cos is None else cos.shape[0] // tm
    in_specs = [pl.BlockSpec((tm, d), lambda i: (i, 0)), pl.BlockSpec((d, SEG_W), lambda i: (0, 0))]
    args = [h, w]
    if cos is not None:
        tab = pl.BlockSpec((tm, LANES), lambda i: (i % n_tab, 0))
        in_specs += [tab, tab]
        args += [cos, sin]
    t_spec = pl.BlockSpec((N_GROUPS, tm, LANES), lambda i: (0, i, 0))
    t_shape = jax.ShapeDtypeStruct((N_GROUPS, rows, LANES), BF16)
    with_f32 = mode in ("kv", "k_rope", "k_rope_t", "v_blocks")
    body = {"kv": _proj_kv_kernel, "q": _proj_q_kernel, "k_rope": _proj_k_rope_kernel,
            "k_rope_t": _proj_k_rope_t_kernel, "q_rope": _proj_q_rope_kernel,
            "v_blocks": functools.partial(_proj_v_blocks_kernel, tk=tk)}[mode]
    if mode == "k_rope":
        f_spec = pl.BlockSpec((tm, SEG_W), lambda i: (i, 0))
        f_shape = jax.ShapeDtypeStruct((rows, SEG_W), F32)
    elif mode == "k_rope_t":
        f_spec = pl.BlockSpec((1, 2 * HA, HD_A, tm), lambda i: (i // n_tab, 0, 0, i % n_tab))
        f_shape = jax.ShapeDtypeStruct((rows // tk, 2 * HA, HD_A, tk), F32)
    else:
        f_spec = pl.BlockSpec((tm, N_GROUPS, LANES), lambda i: (i, 0, 0))
        f_shape = jax.ShapeDtypeStruct((rows, N_GROUPS, LANES), F32)
    if mode == "v_blocks":
        t_spec = pl.BlockSpec((N_GROUPS, tm // tk, LANES + ONES_ROWS, tk), lambda i: (0, i, 0, 0))
        t_shape = jax.ShapeDtypeStruct((N_GROUPS, rows // tk, LANES + ONES_ROWS, tk), BF16)
    out = pl.pallas_call(
        body,
        grid=(rows // tm,),
        in_specs=in_specs,
        out_specs=[f_spec, t_spec] if with_f32 else t_spec,
        out_shape=[f_shape, t_shape] if with_f32 else t_shape,
        compiler_params=_params("parallel"),
        name="proj_" + mode,
    )(*args)
    return (out[0], out[1]) if with_f32 else (None, out)


def _rope_tables(pos):
    half = HD_A // 2
    inv = ROPE_THETA ** (-jnp.arange(half, dtype=F32) / half)
    ang = pos.astype(F32)[:, None] * inv[None, :]
    cos, sin = jnp.cos(ang), jnp.sin(ang)
    reps = LANES // HD_A
    return (jnp.tile(jnp.concatenate([cos, cos], axis=-1), (1, reps)),
            jnp.tile(jnp.concatenate([-sin, sin], axis=-1), (1, reps)))


def _head_lanes(q, c):
    lane = lax.broadcasted_iota(jnp.int32, (1, LANES), 1)
    return jnp.where((lane >= c * HD_A) & (lane < (c + 1) * HD_A), q, jnp.zeros_like(q))


def _suffix_matrix(n):
    j = lax.broadcasted_iota(jnp.int32, (2 * n, n), 0) % n
    s = lax.broadcasted_iota(jnp.int32, (2 * n, n), 1)
    return jnp.where(j >= s, 1.0, 0.0).astype(BF16)


def _log_keep_terms(z, mask):
    neg = -z
    log_keep = jnp.minimum(neg, 0.0) - jnp.log(1.0 + jnp.exp(jnp.minimum(z, neg)))
    if mask is not None:
        log_keep = jnp.where(mask, log_keep, 0.0)
    hi = log_keep.astype(BF16)
    return jnp.concatenate([hi, (log_keep - hi.astype(F32)).astype(BF16)], axis=1)


def _stick_block(z, mask, carry, v, acc_ref, suffix_mat):
    suffix = _dot(_log_keep_terms(z, mask), suffix_mat)
    a = jnp.exp(z + suffix + carry)
    if mask is not None:
        a = jnp.where(mask, a, 0.0)
    acc_ref[...] += _dot(a.astype(BF16), v)
    return carry + suffix[:, 0:1]


def _lambda(lam_ref):
    lam = lam_ref[...]
    d1 = jnp.sum(lam[0:1] * lam[1:2], axis=-1, keepdims=True)
    d2 = jnp.sum(lam[2:3] * lam[3:4], axis=-1, keepdims=True)
    return jnp.exp(d1) - jnp.exp(d2) + LAM_INIT


def _diff_finish(o1, o2, lam, g_head):
    o = o1 - lam * o2
    o = o * lax.rsqrt(jnp.mean(o * o, axis=-1, keepdims=True) + EPS)
    return (o * g_head * (1.0 - LAM_INIT)).astype(BF16)


def _key_norm_bound(k_ref, c, seq, tq):
    rows = _tile(seq, 2048, 16)
    lane = lax.broadcasted_iota(jnp.int32, (8, LANES), 1)
    ones_c = jnp.where((lane >= c * HD_A) & (lane < (c + 1) * HD_A), 1.0, 0.0).astype(BF16)

    def body(i, best):
        k = k_ref[0, pl.ds(pl.multiple_of(i * rows, rows), rows), :].astype(F32)
        return jnp.maximum(best, _dot_nt(ones_c, (k * k).astype(BF16)))

    best = lax.fori_loop(0, seq // rows, body, jnp.zeros((8, rows), F32))
    return jnp.broadcast_to(jnp.max(best[0:1], axis=1, keepdims=True), (1, tq))


def _diff_prompt_kernel(q1_ref, q2_ref, k1_ref, k2_ref, vt_ref, lam_ref, g_ref, o_ref, acc_ref, s_ref, p_ref,
                        kmax_ref, *, tq, tk, seq):
    qi = pl.program_id(2)
    lam = _lambda(lam_ref)

    @pl.when(qi == 0)
    def _():
        for idx, (c, k_ref) in enumerate((c, k_ref) for c in range(2) for k_ref in (k1_ref, k2_ref)):
            kmax_ref[idx] = _key_norm_bound(k_ref, c, seq, tq)

    n_full = (qi * tq) // tk
    n_edge = tq // tk
    q_pos = qi * tq + lax.broadcasted_iota(jnp.int32, (1, tq), 1)
    limit = (q_pos // CHUNK + 1) * CHUNK
    k_pos = n_full * tk + lax.broadcasted_iota(jnp.int32, (tk, 1), 0)
    edge_masks = [k_pos + e * tk < limit for e in range(n_edge)]

    streams = [(c, _head_lanes(q_ref[0], c), k_ref)
               for c in range(2) for q_ref, k_ref in ((q1_ref, k1_ref), (q2_ref, k2_ref))]
    n_streams = len(streams)

    def score_block(j):
        start = pl.multiple_of(j * tk, tk)
        return [_dot_nt(k_ref[0, pl.ds(start, tk), :], q) for _, q, k_ref in streams]

    def after(x):
        return 0.0 * x

    def store_scores(scores, readers=None):
        for idx, s in enumerate(scores):
            s_ref[idx] = s if readers is None else s + readers[idx]

    def value_block(j, alphas, slot=1):
        done = []
        for idx, (c, _, _) in enumerate(streams):
            pv = _dot(vt_ref[c, j], p_ref[slot, idx])
            acc_ref[idx] = pv + (acc_ref[idx] if alphas is None else alphas[idx] * acc_ref[idx])
            done.append(after(pv[VD_A:VD_A + 1, :]))
        return done

    def masked(s, mask):
        return s if mask is None else jnp.where(mask, s, NEG)

    ones8 = jnp.ones((8, LANES), BF16)
    bounds = []
    for idx, (_, q, _) in enumerate(streams):
        qf = q.astype(F32)
        q_sq = _dot_nt(ones8, (qf * qf).astype(BF16))[0:1]
        bounds.append(jnp.sqrt(q_sq * kmax_ref[idx]) * 1.02)

    def store_weights(scores, mask, slot, readers=None):
        for idx, s in enumerate(scores):
            shift = bounds[idx] if readers is None else bounds[idx] + readers[idx]
            p_ref[slot, idx] = jnp.exp(masked(s, mask) - shift).astype(BF16)

    def bound_block(j, mask):
        scores = score_block(j)
        done = value_block(jnp.maximum(j - 1, 0), None)
        store_weights(scores, mask, 1, done)

    def bound_blocks(j, masks):
        n = len(masks)
        scores = score_block(j)
        done = {1: value_block(jnp.maximum(j - 1, 0), None), 0: None}
        for u, mask in enumerate(masks):
            slot = u % 2
            next_scores = score_block(j + u + 1) if u < n - 1 else None
            store_weights(scores, mask, slot, done[slot])
            if u < n - 1:
                done[slot] = value_block(j + u, None, slot=slot)
            scores = next_scores

    acc_ref[...] = jnp.zeros_like(acc_ref)
    p_ref[1] = jnp.zeros_like(p_ref[1])
    first = 0
    for n in TRIP_BLOCKS:
        trips = (n_full - first) // n
        lax.fori_loop(first // n, first // n + trips, lambda i, _, n=n: bound_blocks(n * i, [None] * n), None)
        first = first + trips * n

    @pl.when(n_full % 2 == 1)
    def _():
        bound_block(n_full - 1, None)

    if n_edge % 2 == 0:
        bound_blocks(n_full, edge_masks)
    else:
        for e, mask in enumerate(edge_masks):
            bound_block(n_full + e, mask)
    value_block(n_full + n_edge - 1, None)

    denom_min = acc_ref[0, VD_A:VD_A + 1, :]
    for idx in range(1, n_streams):
        denom_min = jnp.minimum(denom_min, acc_ref[idx, VD_A:VD_A + 1, :])

    @pl.when(jnp.logical_not(jnp.min(denom_min) >= DENOM_MIN))
    def _():
        def max_block(maxima, mask, done):
            new_maxima, alphas = [], []
            for idx in range(n_streams):
                s = masked(s_ref[idx], mask)
                m_new = jnp.maximum(maxima[idx], jnp.max(s, axis=0, keepdims=True))
                alphas.append(jnp.exp(maxima[idx] - m_new))
                new_maxima.append(m_new)
                p_ref[1, idx] = jnp.exp(s - (m_new + done[idx])).astype(BF16)
            return tuple(new_maxima), tuple(alphas)

        def step(j, carry):
            maxima, alphas = carry
            next_scores = score_block(j + 1)
            done = value_block(jnp.maximum(j - 1, 0), alphas)
            maxima, alphas = max_block(maxima, None, done)
            store_scores(next_scores, [after(m) for m in maxima])
            return maxima, alphas

        acc_ref[...] = jnp.zeros_like(acc_ref)
        p_ref[...] = jnp.zeros_like(p_ref)
        store_scores(score_block(0))
        maxima = tuple(jnp.full((1, tq), NEG, F32) for _ in streams)
        alphas = tuple(jnp.ones((1, tq), F32) for _ in streams)
        maxima, alphas = lax.fori_loop(0, n_full, step, (maxima, alphas))
        done = value_block(jnp.maximum(n_full - 1, 0), alphas)
        for e, mask in enumerate(edge_masks):
            if e > 0:
                store_scores(score_block(n_full + e), [after(m) for m in maxima])
            maxima, alphas = max_block(maxima, mask, done)
            done = value_block(n_full + e, alphas)

    for c in range(2):
        a1, a2 = acc_ref[2 * c], acc_ref[2 * c + 1]
        o = a1[:VD_A] / a1[VD_A:VD_A + 1] - lam * (a2[:VD_A] / a2[VD_A:VD_A + 1])
        o = o * lax.rsqrt(jnp.mean(o * o, axis=0, keepdims=True) + EPS)
        o = o * g_ref[...] * (1.0 - LAM_INIT)
        o_ref[:, c * VD_A:(c + 1) * VD_A] = o.T.astype(BF16)


def _diff_prompt(qa_t, ka_t, va_tt, lam_vecs, g_head, batch, seq, tk):
    tq = 2 * tk if seq % (2 * tk) == 0 else tk
    nq = seq // tq
    half = N_GROUPS // 2
    q_spec = lambda off: pl.BlockSpec((1, tq, LANES), lambda b, p, i: (p + off, b * nq + i, 0))
    k_spec = lambda off: pl.BlockSpec((1, seq, LANES), lambda b, p, i: (p + off, b, 0))
    return pl.pallas_call(
        functools.partial(_diff_prompt_kernel, tq=tq, tk=tk, seq=seq),
        grid=(batch, half, nq),
        in_specs=[q_spec(0), q_spec(half), k_spec(0), k_spec(half),
                  pl.BlockSpec((2, seq // tk, VD_A + ONES_ROWS, tk), lambda b, p, i: (p, b, 0, 0)),
                  pl.BlockSpec((4, HD_A), lambda b, p, i: (0, 0)),
                  pl.BlockSpec((VD_A, 1), lambda b, p, i: (0, 0))],
        out_specs=pl.BlockSpec((tq, 2 * VD_A), lambda b, p, i: (b * nq + i, p)),
        out_shape=jax.ShapeDtypeStruct((batch * seq, SEG_W), BF16),
        scratch_shapes=[pltpu.VMEM((4, VD_A + ONES_ROWS, tq), F32), pltpu.VMEM((4, tk, tq), F32),
                        pltpu.VMEM((2, 4, tk, tq), BF16), pltpu.VMEM((4, 1, tq), F32)],
        compiler_params=_params("arbitrary", "arbitrary", "arbitrary"),
        name="diff_prompt",
    )(qa_t, qa_t, ka_t, ka_t, va_tt, lam_vecs, g_head.reshape(VD_A, 1))


def _stick_blocks(zs, mask, carries, vs, acc_ref, suffix_mat):
    terms = [_log_keep_terms(z, mask) for z in zs]
    suffixes = [_dot(t, suffix_mat) for t in terms]
    weights = []
    for z, suffix, carry in zip(zs, suffixes, carries):
        a = jnp.exp(z + suffix + carry)
        if mask is not None:
            a = jnp.where(mask, a, 0.0)
        weights.append(a.astype(BF16))
    for idx, (a, v) in enumerate(zip(weights, vs)):
        acc_ref[idx] += _dot(a, v)
    return tuple(carry + suffix[:, 0:1] for carry, suffix in zip(carries, suffixes))


def _stick_prompt_kernel(q_ref, k_ref, v_ref, o_ref, acc_ref, *, tq, tk, heads):
    qi = pl.program_id(2)
    qs = [q_ref[h] for h in range(heads)]
    suffix_mat = _suffix_matrix(tk)
    scale = HD_B ** -0.5
    acc_ref[...] = jnp.zeros_like(acc_ref)
    q_pos = qi * tq + lax.broadcasted_iota(jnp.int32, (tq, 1), 0)

    def block(start, mask, carries):
        zs = [_dot_nt(qs[h], k_ref[h, pl.ds(start, tk), :]) * scale for h in range(heads)]
        vs = [v_ref[h, pl.ds(start, tk), :] for h in range(heads)]
        return _stick_blocks(zs, mask, carries, vs, acc_ref, suffix_mat)

    carries = tuple(jnp.zeros((tq, 1), F32) for _ in range(heads))
    for e in range(tq // tk - 1, -1, -1):
        start = pl.multiple_of(qi * tq + e * tk, tk)
        k_pos = start + lax.broadcasted_iota(jnp.int32, (1, tk), 1)
        carries = block(start, k_pos < q_pos, carries)

    n_full = (qi * tq) // tk

    def cond(st):
        i, carries = st
        largest = functools.reduce(jnp.maximum, [jnp.max(c) for c in carries])
        return jnp.logical_and(i < n_full, largest > UNDERFLOW_LOG)

    def body(st):
        i, carries = st
        return i + 1, block(pl.multiple_of((n_full - 1 - i) * tk, tk), None, carries)

    lax.while_loop(cond, body, (jnp.int32(0), carries))
    for h in range(heads):
        o_ref[:, h * HD_B:(h + 1) * HD_B] = acc_ref[h].astype(BF16)


def _stick_prompt(qb_t, kb_t, vb_t, batch, seq):
    tq = _tile(seq, 256, 8)
    tk = tq
    heads = 4
    nq = seq // tq
    kv_spec = pl.BlockSpec((heads, seq, LANES), lambda b, h, i: (h, b, 0), pipeline_mode=pl.Buffered(1))
    return pl.pallas_call(
        functools.partial(_stick_prompt_kernel, tq=tq, tk=tk, heads=heads),
        grid=(batch, HB // heads, nq),
        in_specs=[pl.BlockSpec((heads, tq, LANES), lambda b, h, i: (h, b * nq + i, 0)), kv_spec, kv_spec],
        out_specs=pl.BlockSpec((tq, heads * HD_B), lambda b, h, i: (b * nq + i, h)),
        out_shape=jax.ShapeDtypeStruct((batch * seq, SEG_W), BF16),
        scratch_shapes=[pltpu.VMEM((heads, tq, HD_B), F32)],
        compiler_params=_params("parallel", "parallel", "arbitrary"),
        name="stick_prompt",
    )(qb_t, kb_t, vb_t)


def _diff_sample_kernel(qa_ref, ka_ref, va_ref, kc_ref, vc_ref, lam_ref, g_ref, o_ref, *, ts, past):
    lam = _lambda(lam_ref)
    q_pos = past + lax.broadcasted_iota(jnp.int32, (ts, 1), 0)
    limit = (q_pos // CHUNK + 1) * CHUNK
    k_pos = past + lax.broadcasted_iota(jnp.int32, (1, ts), 1)
    new_mask = k_pos < limit

    def attend(g, c, v_cache, v_new):
        q = _head_lanes(qa_ref[g], c)
        k_cache = jnp.concatenate([kc_ref[0, 2 * g].astype(BF16), kc_ref[0, 2 * g + 1].astype(BF16)], axis=0)
        s_cache = _dot(q, k_cache)
        s_new = jnp.where(new_mask, _dot_nt(q, ka_ref[g]), NEG)
        m = jnp.maximum(jnp.max(s_cache, axis=-1, keepdims=True), jnp.max(s_new, axis=-1, keepdims=True))
        p_cache = jnp.exp(s_cache - m)
        p_new = jnp.exp(s_new - m)
        l = jnp.sum(p_cache, axis=-1, keepdims=True) + jnp.sum(p_new, axis=-1, keepdims=True)
        return (_dot(p_cache.astype(BF16), v_cache) + _dot(p_new.astype(BF16), v_new)) / l

    half = N_GROUPS // 2
    for h in range(HA):
        g, c = h // 2, h % 2
        v_cache = vc_ref[0, :, h, :].astype(BF16)
        o1 = attend(g, c, v_cache, va_ref[h])
        o2 = attend(g + half, c, v_cache, va_ref[h])
        o_ref[:, h * VD_A:(h + 1) * VD_A] = _diff_finish(o1, o2, lam, g_ref[...])


def _diff_sample(qa_t, ka_t, va_t, cache_k, cache_v, lam_vecs, g_head, batch, ts, past):
    new_spec = pl.BlockSpec((N_GROUPS, ts, LANES), lambda b: (0, b, 0))
    return pl.pallas_call(
        functools.partial(_diff_sample_kernel, ts=ts, past=past),
        grid=(batch,),
        in_specs=[new_spec, new_spec, new_spec,
                  pl.BlockSpec((1, 2 * HA, HD_A, past), lambda b: (b, 0, 0, 0)),
                  pl.BlockSpec((1, past, HA, VD_A), lambda b: (b, 0, 0, 0)),
                  pl.BlockSpec((4, HD_A), lambda b: (0, 0)),
                  pl.BlockSpec((1, VD_A), lambda b: (0, 0))],
        out_specs=pl.BlockSpec((ts, SEG_W), lambda b: (b, 0)),
        out_shape=jax.ShapeDtypeStruct((batch * ts, SEG_W), BF16),
        compiler_params=_params("parallel"),
        name="diff_sample",
    )(qa_t, ka_t, va_t, cache_k, cache_v, lam_vecs, g_head)


def _stick_sample_kernel(q_ref, kn_ref, vn_ref, kc_ref, vc_ref, o_ref, acc_ref, *, ts, past, tk):
    scale = HD_B ** -0.5
    row = lax.broadcasted_iota(jnp.int32, (ts, 1), 0)
    col = lax.broadcasted_iota(jnp.int32, (1, ts), 1)
    new_suffix_mat = _suffix_matrix(ts)
    suffix_mat = _suffix_matrix(tk)
    n_full = past // tk

    for h in range(HB):
        q = q_ref[h]
        acc_ref[...] = jnp.zeros_like(acc_ref)
        z = _dot_nt(q, kn_ref[h]) * scale
        carry = _stick_block(z, col < row, jnp.zeros((ts, 1), F32), vn_ref[h], acc_ref, new_suffix_mat)

        def cond(st):
            i, carry = st
            return jnp.logical_and(i < n_full, jnp.max(carry) > UNDERFLOW_LOG)

        def body(st, q=q, h=h):
            i, carry = st
            start = pl.multiple_of((n_full - 1 - i) * tk, tk)
            k = kc_ref[0, pl.ds(start, tk), h, :].astype(BF16)
            v = vc_ref[0, pl.ds(start, tk), h, :].astype(BF16)
            return i + 1, _stick_block(_dot_nt(q, k) * scale, None, carry, v, acc_ref, suffix_mat)

        lax.while_loop(cond, body, (jnp.int32(0), carry))
        o_ref[:, h * HD_B:(h + 1) * HD_B] = acc_ref[...].astype(BF16)


def _stick_sample(qb_t, kb_t, vb_t, cache_k, cache_v, batch, ts, past):
    tk = _tile(past, 256, 8)
    new_spec = pl.BlockSpec((N_GROUPS, ts, LANES), lambda b: (0, b, 0))
    cache_spec = pl.BlockSpec((1, past, HB, HD_B), lambda b: (b, 0, 0, 0))
    return pl.pallas_call(
        functools.partial(_stick_sample_kernel, ts=ts, past=past, tk=tk),
        grid=(batch,),
        in_specs=[new_spec, new_spec, new_spec, cache_spec, cache_spec],
        out_specs=pl.BlockSpec((ts, SEG_W), lambda b: (b, 0)),
        out_shape=jax.ShapeDtypeStruct((batch * ts, SEG_W), BF16),
        scratch_shapes=[pltpu.VMEM((ts, HD_B), F32)],
        compiler_params=_params("parallel"),
        name="stick_sample",
    )(qb_t, kb_t, vb_t, cache_k, cache_v)


def _merge_kernel(h_ref, oa_ref, ob_ref, wga_ref, wgb_ref, wpa_ref, wpb_ref, o_ref):
    h = h_ref[...]
    ga = jax.nn.sigmoid(_dot(h, wga_ref[...]))
    gb = jax.nn.sigmoid(_dot(h, wgb_ref[...]))
    ya = _dot(oa_ref[...], wpa_ref[...])
    yb = _dot(ob_ref[...], wpb_ref[...])
    o_ref[...] = (ga * ya + gb * yb).astype(BF16)


def _merge(h, oa, ob, w_gate, w_proj_a, w_proj_b):
    rows, d = h.shape
    tm = _tile(rows, 512)
    tn = 512
    nn = d // tn
    row_spec = lambda w: pl.BlockSpec((tm, w), lambda i, n: (i, 0))
    return pl.pallas_call(
        _merge_kernel,
        grid=(rows // tm, nn),
        in_specs=[row_spec(d), row_spec(SEG_W), row_spec(SEG_W),
                  pl.BlockSpec((d, tn), lambda i, n: (0, n)),
                  pl.BlockSpec((d, tn), lambda i, n: (0, nn + n)),
                  pl.BlockSpec((SEG_W, tn), lambda i, n: (0, n)),
                  pl.BlockSpec((SEG_W, tn), lambda i, n: (0, n))],
        out_specs=pl.BlockSpec((tm, tn), lambda i, n: (i, n)),
        out_shape=jax.ShapeDtypeStruct((rows, d), BF16),
        compiler_params=_params("parallel", "arbitrary"),
        name="merge",
    )(h, oa, ob, w_gate, w_gate, w_proj_a, w_proj_b)


def _out_kernel(x_ref, m_ref, w_ref, g_post_ref, g_pre_ref, x1_ref, hm_ref):
    x1 = x_ref[...] + _rms(_dot(m_ref[...], w_ref[...]), g_post_ref[...])
    x1_ref[...] = x1
    hm_ref[...] = _rms(x1, g_pre_ref[...]).astype(BF16)


def _out_proj(x, merged, w_out, g_post_mix, g_pre_mlp):
    rows, d = x.shape
    tm = _tile(rows, 256)
    row = pl.BlockSpec((tm, d), lambda i: (i, 0))
    vec = pl.BlockSpec((1, d), lambda i: (0, 0))
    return pl.pallas_call(
        _out_kernel,
        grid=(rows // tm,),
        in_specs=[row, row, pl.BlockSpec((d, d), lambda i: (0, 0)), vec, vec],
        out_specs=[row, row],
        out_shape=[jax.ShapeDtypeStruct((rows, d), F32), jax.ShapeDtypeStruct((rows, d), BF16)],
        compiler_params=_params("parallel"),
        name="out_proj",
    )(x, merged, w_out, g_post_mix, g_pre_mlp)


def _mlp_kernel(x1_ref, hm_ref, wu_ref, wd_ref, g_ref, y_ref, acc_ref):
    f = pl.program_id(1)

    @pl.when(f == 0)
    def _():
        acc_ref[...] = jnp.zeros_like(acc_ref)

    hm = hm_ref[...]
    half = wu_ref.shape[1] // 2
    down = []
    for s in range(2):
        u = jnp.square(jnp.maximum(_dot(hm, wu_ref[:, s * half:(s + 1) * half]), 0.0))
        down.append(_dot(u.astype(BF16), wd_ref[s * half:(s + 1) * half, :]))
    acc_ref[...] += down[0] + down[1]

    @pl.when(f == pl.num_programs(1) - 1)
    def _():
        y_ref[...] = x1_ref[...] + _rms(acc_ref[...], g_ref[...])


def _mlp(x1, hm, w_up, w_down, g_post_mlp):
    rows, d = x1.shape
    d_ff = w_up.shape[1]
    tm = _tile(rows, 512)
    tf = 1024
    row = pl.BlockSpec((tm, d), lambda i, f: (i, 0))
    return pl.pallas_call(
        _mlp_kernel,
        grid=(rows // tm, d_ff // tf),
        in_specs=[row, row,
                  pl.BlockSpec((d, tf), lambda i, f: (0, f)),
                  pl.BlockSpec((tf, d), lambda i, f: (f, 0)),
                  pl.BlockSpec((1, d), lambda i, f: (0, 0))],
        out_specs=row,
        out_shape=jax.ShapeDtypeStruct((rows, d), F32),
        scratch_shapes=[pltpu.VMEM((tm, d), F32)],
        compiler_params=_params("parallel", "arbitrary"),
        name="mlp",
    )(x1, hm, w_up, w_down, g_post_mlp)


def _project_all(h, w_segs, cos, sin, v_block=None):
    _, qa_t = _project(h, w_segs[0], "q_rope", cos, sin)
    if v_block is None:
        ka_f, ka_t = _project(h, w_segs[1], "k_rope", cos, sin)
    else:
        ka_f, ka_t = _project(h, w_segs[1], "k_rope_t", cos, sin, tk=cos.shape[0])
    if v_block is None:
        va_f, va_t = _project(h, w_segs[2], "kv")
    else:
        va_f, va_t = _project(h, w_segs[2], "v_blocks", tk=v_block)
    _, qb_t = _project(h, w_segs[3], "q")
    kb_f, kb_t = _project(h, w_segs[4], "kv")
    vb_f, vb_t = _project(h, w_segs[5], "kv")
    return (qa_t, ka_t, va_t, qb_t, kb_t, vb_t), (ka_f, va_f, kb_f, vb_f)


def _tail(x, h, oa, ob, w):
    merged = _merge(h, oa, ob, w["gate"], w["proj_a"], w["proj_b"])
    x1, hm = _out_proj(x, merged, w["out"], w["g_post_mix"], w["g_pre_mlp"])
    return _mlp(x1, hm, w["up"], w["down"], w["g_post_mlp"])


def kernel(x_prompt, x_sample, cache_diff_k, cache_diff_v, cache_sb_k, cache_sb_v, g_pre_mix, w_in, lambda_q1, lambda_k1, lambda_q2, lambda_k2, g_diff_head, w_gate, w_proj_a, w_proj_b, w_out, g_post_mix, g_pre_mlp, w_up, w_down, g_post_mlp):
    bp, tp, d = x_prompt.shape
    bs, ts, _ = x_sample.shape
    depth, _, past = cache_diff_k.shape[:3]
    assert depth == 1 and tp % CHUNK == 0

    w_in_b = w_in[0].astype(BF16)
    w_segs = [w_in_b[:, s * SEG_W:(s + 1) * SEG_W] for s in range(6)]
    w = dict(gate=w_gate[0].astype(BF16), proj_a=w_proj_a[0].astype(BF16), proj_b=w_proj_b[0].astype(BF16),
             out=w_out[0].astype(BF16), up=w_up[0].astype(BF16), down=w_down[0].astype(BF16),
             g_post_mix=g_post_mix, g_pre_mlp=g_pre_mlp, g_post_mlp=g_post_mlp)
    lam_vecs = jnp.concatenate([lambda_q1, lambda_k1, lambda_q2, lambda_k2], axis=0).astype(F32)

    xp = x_prompt.reshape(bp * tp, d)
    hp = _prenorm(xp, g_pre_mix)
    cos_p, sin_p = _rope_tables(jnp.arange(tp, dtype=jnp.int32))
    tk_diff = _tile(tp, 256, CHUNK)
    (qa_t, ka_t, va_tt, qb_t, kb_t, vb_t), kv_p = _project_all(hp, w_segs, cos_p, sin_p, v_block=tk_diff)
    oa = _diff_prompt(qa_t, ka_t, va_tt, lam_vecs, g_diff_head, bp, tp, tk_diff)
    ob = _stick_prompt(qb_t, kb_t, vb_t, bp, tp)
    yp = _tail(xp, hp, oa, ob, w).reshape(bp, tp, d)

    xs = x_sample.reshape(bs * ts, d)
    hs = _prenorm(xs, g_pre_mix)
    pos_s = past + jnp.arange(ts, dtype=jnp.int32)
    cos_s, sin_s = _rope_tables(jnp.tile(pos_s, bs))
    (qa_t, ka_t, va_t, qb_t, kb_t, vb_t), kv_s = _project_all(hs, w_segs, cos_s, sin_s)
    oa = _diff_sample(qa_t, ka_t, va_t, jnp.transpose(cache_diff_k[0], (0, 2, 3, 1)), cache_diff_v[0],
                      lam_vecs, g_diff_head, bs, ts, past)
    ob = _stick_sample(qb_t, kb_t, vb_t, cache_sb_k[0], cache_sb_v[0], bs, ts, past)
    ys = _tail(xs, hs, oa, ob, w).reshape(bs, ts, d)

    def caches(kv, b, t):
        ka_f, va_f, kb_f, vb_f = kv
        if ka_f.ndim == 4:
            ka_f = jnp.transpose(ka_f, (0, 3, 1, 2))
        return (ka_f.reshape(1, b, t, 2 * HA, HD_A), va_f.reshape(1, b, t, HA, VD_A),
                kb_f.reshape(1, b, t, HB, HD_B), vb_f.reshape(1, b, t, HB, HD_B))

    return (yp, ys) + caches(kv_p, bp, tp) + caches(kv_s, bs, ts)
```

```python
import functools
import math

import jax
import jax.numpy as jnp
from jax import lax
from jax.experimental import pallas as pl
from jax.experimental.pallas import tpu as pltpu

F32 = jnp.float32
BF16 = jnp.bfloat16

CHUNK = 64
HA = 8
HD_A = 64
VD_A = 128
HB = 8
HD_B = 128
SEG_W = 1024
LANES = 128
N_GROUPS = SEG_W // LANES
ROPE_THETA = 10000.0
EPS = 1e-6
NEG = -1e30
LAM_INIT = 0.8 - 0.6 * math.exp(-0.3 * 0)
UNDERFLOW_LOG = -104.0
DENOM_MIN = 1e-20
TRIP_BLOCKS = (16, 8, 4, 2)
ONES_ROWS = 16
VMEM_LIMIT = 56 * 1024 * 1024


def _tile(n, pref, mult=8):
    if n <= pref:
        return n
    for t in range(pref, 0, -1):
        if n % t == 0 and t % mult == 0:
            return t
    raise ValueError(f"no tile for {n}")


def _params(*sem):
    return pltpu.CompilerParams(dimension_semantics=sem, vmem_limit_bytes=VMEM_LIMIT)


def _dot(a, b):
    return jnp.dot(a, b, preferred_element_type=F32)


def _dot_nt(a, b):
    return lax.dot_general(a, b, (((1,), (1,)), ((), ())), preferred_element_type=F32)


def _rms(x, g):
    return x * lax.rsqrt(jnp.mean(x * x, axis=-1, keepdims=True) + EPS) * g


def _prenorm_kernel(x_ref, g_ref, h_ref):
    h_ref[...] = _rms(x_ref[...], g_ref[...]).astype(BF16)


def _prenorm(x, g):
    rows, d = x.shape
    tm = _tile(rows, 512)
    return pl.pallas_call(
        _prenorm_kernel,
        grid=(rows // tm,),
        in_specs=[pl.BlockSpec((tm, d), lambda i: (i, 0)), pl.BlockSpec((1, d), lambda i: (0, 0))],
        out_specs=pl.BlockSpec((tm, d), lambda i: (i, 0)),
        out_shape=jax.ShapeDtypeStruct((rows, d), BF16),
        compiler_params=_params("parallel"),
        name="prenorm",
    )(x, g)


PROJ_CHUNK = 2


def _project_chunks(h_ref, w_ref):
    h = h_ref[...]
    for c0 in range(0, N_GROUPS, PROJ_CHUNK):
        yield c0, _dot(h, w_ref[:, c0 * LANES:(c0 + PROJ_CHUNK) * LANES])


def _groups(z):
    return [z[:, c * LANES:(c + 1) * LANES] for c in range(z.shape[1] // LANES)]


def _store_groups(t_ref, z, c0):
    for c, zc in enumerate(_groups(z)):
        t_ref[c0 + c] = zc.astype(BF16)


def _rope_groups(z, cos, sin):
    lane = lax.broadcasted_iota(jnp.int32, (1, LANES), 1)
    first_half = (lane % HD_A) < (HD_A // 2)
    out = []
    for zc in _groups(z):
        partner = jnp.where(first_half, pltpu.roll(zc, LANES - HD_A // 2, 1), pltpu.roll(zc, HD_A // 2, 1))
        out.append(zc * cos + partner * sin)
    return out


def _store_heads(f_ref, z, c0):
    for c, zc in enumerate(_groups(z)):
        f_ref[:, c0 + c, :] = zc


def _proj_kv_kernel(h_ref, w_ref, f_ref, t_ref):
    for c0, z in _project_chunks(h_ref, w_ref):
        _store_heads(f_ref, z, c0)
        _store_groups(t_ref, z, c0)


def _proj_v_blocks_kernel(h_ref, w_ref, f_ref, t_ref, *, tk):
    for c0, z in _project_chunks(h_ref, w_ref):
        _store_heads(f_ref, z, c0)
        for c, zc in enumerate(_groups(z)):
            zt = zc.T
            for j in range(zt.shape[1] // tk):
                t_ref[c0 + c, j, :LANES, :] = zt[:, j * tk:(j + 1) * tk].astype(BF16)
    row = lax.broadcasted_iota(jnp.int32, (ONES_ROWS, tk), 0)
    ones_rows = jnp.where(row == 0, 1.0, 0.0).astype(BF16)
    for c in range(N_GROUPS):
        for j in range(t_ref.shape[1]):
            t_ref[c, j, LANES:, :] = ones_rows


def _proj_q_kernel(h_ref, w_ref, t_ref):
    for c0, z in _project_chunks(h_ref, w_ref):
        _store_groups(t_ref, z, c0)


def _proj_k_rope_kernel(h_ref, w_ref, cos_ref, sin_ref, f_ref, t_ref):
    for c0, z in _project_chunks(h_ref, w_ref):
        for c, zc in enumerate(_rope_groups(z, cos_ref[...], sin_ref[...]), c0):
            f_ref[:, c * LANES:(c + 1) * LANES] = zc
            t_ref[c] = zc.astype(BF16)


def _proj_k_rope_t_kernel(h_ref, w_ref, cos_ref, sin_ref, f_ref, t_ref):
    for c0, z in _project_chunks(h_ref, w_ref):
        for c, zc in enumerate(_rope_groups(z, cos_ref[...], sin_ref[...]), c0):
            zt = zc.T
            f_ref[0, 2 * c] = zt[:HD_A]
            f_ref[0, 2 * c + 1] = zt[HD_A:]
            t_ref[c] = zc.astype(BF16)


def _proj_q_rope_kernel(h_ref, w_ref, cos_ref, sin_ref, t_ref):
    for c0, z in _project_chunks(h_ref, w_ref):
        for c, zc in enumerate(_rope_groups(z, cos_ref[...], sin_ref[...]), c0):
            t_ref[c] = (zc * (HD_A ** -0.5)).astype(BF16)


def _project(h, w, mode, cos=None, sin=None, tk=None):
    rows, d = h.shape
    tm = _tile(rows, 1024)
    n_tab = None if cos is None else cos.shape[0] // tm
    in_specs = [pl.BlockSpec((tm, d), lambda i: (i, 0)), pl.BlockSpec((d, SEG_W), lambda i: (0, 0))]
    args = [h, w]
    if cos is not None:
        tab = pl.BlockSpec((tm, LANES), lambda i: (i % n_tab, 0))
        in_specs += [tab, tab]
        args += [cos, sin]
    t_spec = pl.BlockSpec((N_GROUPS, tm, LANES), lambda i: (0, i, 0))
    t_shape = jax.ShapeDtypeStruct((N_GROUPS, rows, LANES), BF16)
    with_f32 = mode in ("kv", "k_rope", "k_rope_t", "v_blocks")
    body = {"kv": _proj_kv_kernel, "q": _proj_q_kernel, "k_rope": _proj_k_rope_kernel,
            "k_rope_t": _proj_k_rope_t_kernel, "q_rope": _proj_q_rope_kernel,
            "v_blocks": functools.partial(_proj_v_blocks_kernel, tk=tk)}[mode]
    if mode == "k_rope":
        f_spec = pl.BlockSpec((tm, SEG_W), lambda i: (i, 0))
        f_shape = jax.ShapeDtypeStruct((rows, SEG_W), F32)
    elif mode == "k_rope_t":
        f_spec = pl.BlockSpec((1, 2 * HA, HD_A, tm), lambda i: (i // n_tab, 0, 0, i % n_tab))
        f_shape = jax.ShapeDtypeStruct((rows // tk, 2 * HA, HD_A, tk), F32)
    else:
        f_spec = pl.BlockSpec((tm, N_GROUPS, LANES), lambda i: (i, 0, 0))
        f_shape = jax.ShapeDtypeStruct((rows, N_GROUPS, LANES), F32)
    if mode == "v_blocks":
        t_spec = pl.BlockSpec((N_GROUPS, tm // tk, LANES + ONES_ROWS, tk), lambda i: (0, i, 0, 0))
        t_shape = jax.ShapeDtypeStruct((N_GROUPS, rows // tk, LANES + ONES_ROWS, tk), BF16)
    out = pl.pallas_call(
        body,
        grid=(rows // tm,),
        in_specs=in_specs,
        out_specs=[f_spec, t_spec] if with_f32 else t_spec,
        out_shape=[f_shape, t_shape] if with_f32 else t_shape,
        compiler_params=_params("parallel"),
        name="proj_" + mode,
    )(*args)
    return (out[0], out[1]) if with_f32 else (None, out)


def _rope_tables(pos):
    half = HD_A // 2
    inv = ROPE_THETA ** (-jnp.arange(half, dtype=F32) / half)
    ang = pos.astype(F32)[:, None] * inv[None, :]
    cos, sin = jnp.cos(ang), jnp.sin(ang)
    reps = LANES // HD_A
    return (jnp.tile(jnp.concatenate([cos, cos], axis=-1), (1, reps)),
            jnp.tile(jnp.concatenate([-sin, sin], axis=-1), (1, reps)))


def _head_lanes(q, c):
    lane = lax.broadcasted_iota(jnp.int32, (1, LANES), 1)
    return jnp.where((lane >= c * HD_A) & (lane < (c + 1) * HD_A), q, jnp.zeros_like(q))


def _suffix_matrix(n):
    j = lax.broadcasted_iota(jnp.int32, (2 * n, n), 0) % n
    s = lax.broadcasted_iota(jnp.int32, (2 * n, n), 1)
    return jnp.where(j >= s, 1.0, 0.0).astype(BF16)


def _log_keep_terms(z, mask):
    neg = -z
    log_keep = jnp.minimum(neg, 0.0) - jnp.log(1.0 + jnp.exp(jnp.minimum(z, neg)))
    if mask is not None:
        log_keep = jnp.where(mask, log_keep, 0.0)
    hi = log_keep.astype(BF16)
    return jnp.concatenate([hi, (log_keep - hi.astype(F32)).astype(BF16)], axis=1)


def _stick_block(z, mask, carry, v, acc_ref, suffix_mat):
    suffix = _dot(_log_keep_terms(z, mask), suffix_mat)
    a = jnp.exp(z + suffix + carry)
    if mask is not None:
        a = jnp.where(mask, a, 0.0)
    acc_ref[...] += _dot(a.astype(BF16), v)
    return carry + suffix[:, 0:1]


def _lambda(lam_ref):
    lam = lam_ref[...]
    d1 = jnp.sum(lam[0:1] * lam[1:2], axis=-1, keepdims=True)
    d2 = jnp.sum(lam[2:3] * lam[3:4], axis=-1, keepdims=True)
    return jnp.exp(d1) - jnp.exp(d2) + LAM_INIT


def _diff_finish(o1, o2, lam, g_head):
    o = o1 - lam * o2
    o = o * lax.rsqrt(jnp.mean(o * o, axis=-1, keepdims=True) + EPS)
    return (o * g_head * (1.0 - LAM_INIT)).astype(BF16)


def _key_norm_bound(k_ref, c, seq, tq):
    rows = _tile(seq, 2048, 16)
    lane = lax.broadcasted_iota(jnp.int32, (8, LANES), 1)
    ones_c = jnp.where((lane >= c * HD_A) & (lane < (c + 1) * HD_A), 1.0, 0.0).astype(BF16)

    def body(i, best):
        k = k_ref[0, pl.ds(pl.multiple_of(i * rows, rows), rows), :].astype(F32)
        return jnp.maximum(best, _dot_nt(ones_c, (k * k).astype(BF16)))

    best = lax.fori_loop(0, seq // rows, body, jnp.zeros((8, rows), F32))
    return jnp.broadcast_to(jnp.max(best[0:1], axis=1, keepdims=True), (1, tq))


def _diff_prompt_kernel(q1_ref, q2_ref, k1_ref, k2_ref, vt_ref, lam_ref, g_ref, o_ref, acc_ref, s_ref, p_ref,
                        kmax_ref, *, tq, tk, seq):
    qi = pl.program_id(2)
    lam = _lambda(lam_ref)

    @pl.when(qi == 0)
    def _():
        for idx, (c, k_ref) in enumerate((c, k_ref) for c in range(2) for k_ref in (k1_ref, k2_ref)):
            kmax_ref[idx] = _key_norm_bound(k_ref, c, seq, tq)

    n_full = (qi * tq) // tk
    n_edge = tq // tk
    q_pos = qi * tq + lax.broadcasted_iota(jnp.int32, (1, tq), 1)
    limit = (q_pos // CHUNK + 1) * CHUNK
    k_pos = n_full * tk + lax.broadcasted_iota(jnp.int32, (tk, 1), 0)
    edge_masks = [k_pos + e * tk < limit for e in range(n_edge)]

    streams = [(c, _head_lanes(q_ref[0], c), k_ref)
               for c in range(2) for q_ref, k_ref in ((q1_ref, k1_ref), (q2_ref, k2_ref))]
    n_streams = len(streams)

    def score_block(j):
        start = pl.multiple_of(j * tk, tk)
        return [_dot_nt(k_ref[0, pl.ds(start, tk), :], q) for _, q, k_ref in streams]

    def after(x):
        return 0.0 * x

    def store_scores(scores, readers=None):
        for idx, s in enumerate(scores):
            s_ref[idx] = s if readers is None else s + readers[idx]

    def value_block(j, alphas, slot=1):
        done = []
        for idx, (c, _, _) in enumerate(streams):
            pv = _dot(vt_ref[c, j], p_ref[slot, idx])
            acc_ref[idx] = pv + (acc_ref[idx] if alphas is None else alphas[idx] * acc_ref[idx])
            done.append(after(pv[VD_A:VD_A + 1, :]))
        return done

    def masked(s, mask):
        return s if mask is None else jnp.where(mask, s, NEG)

    ones8 = jnp.ones((8, LANES), BF16)
    bounds = []
    for idx, (_, q, _) in enumerate(streams):
        qf = q.astype(F32)
        q_sq = _dot_nt(ones8, (qf * qf).astype(BF16))[0:1]
        bounds.append(jnp.sqrt(q_sq * kmax_ref[idx]) * 1.02)

    def store_weights(scores, mask, slot, readers=None):
        for idx, s in enumerate(scores):
            shift = bounds[idx] if readers is None else bounds[idx] + readers[idx]
            p_ref[slot, idx] = jnp.exp(masked(s, mask) - shift).astype(BF16)

    def bound_block(j, mask):
        scores = score_block(j)
        done = value_block(jnp.maximum(j - 1, 0), None)
        store_weights(scores, mask, 1, done)

    def bound_blocks(j, masks):
        n = len(masks)
        scores = score_block(j)
        done = {1: value_block(jnp.maximum(j - 1, 0), None), 0: None}
        for u, mask in enumerate(masks):
            slot = u % 2
            next_scores = score_block(j + u + 1) if u < n - 1 else None
            store_weights(scores, mask, slot, done[slot])
            if u < n - 1:
                done[slot] = value_block(j + u, None, slot=slot)
            scores = next_scores

    acc_ref[...] = jnp.zeros_like(acc_ref)
    p_ref[1] = jnp.zeros_like(p_ref[1])
    first = 0
    for n in TRIP_BLOCKS:
        trips = (n_full - first) // n
        lax.fori_loop(first // n, first // n + trips, lambda i, _, n=n: bound_blocks(n * i, [None] * n), None)
        first = first + trips * n

    @pl.when(n_full % 2 == 1)
    def _():
        bound_block(n_full - 1, None)

    if n_edge % 2 == 0:
        bound_blocks(n_full, edge_masks)
    else:
        for e, mask in enumerate(edge_masks):
            bound_block(n_full + e, mask)
    value_block(n_full + n_edge - 1, None)

    denom_min = acc_ref[0, VD_A:VD_A + 1, :]
    for idx in range(1, n_streams):
        denom_min = jnp.minimum(denom_min, acc_ref[idx, VD_A:VD_A + 1, :])

    @pl.when(jnp.logical_not(jnp.min(denom_min) >= DENOM_MIN))
    def _():
        def max_block(maxima, mask, done):
            new_maxima, alphas = [], []
            for idx in range(n_streams):
                s = masked(s_ref[idx], mask)
                m_new = jnp.maximum(maxima[idx], jnp.max(s, axis=0, keepdims=True))
                alphas.append(jnp.exp(maxima[idx] - m_new))
                new_maxima.append(m_new)
                p_ref[1, idx] = jnp.exp(s - (m_new + done[idx])).astype(BF16)
            return tuple(new_maxima), tuple(alphas)

        def step(j, carry):
            maxima, alphas = carry
            next_scores = score_block(j + 1)
            done = value_block(jnp.maximum(j - 1, 0), alphas)
            maxima, alphas = max_block(maxima, None, done)
            store_scores(next_scores, [after(m) for m in maxima])
            return maxima, alphas

        acc_ref[...] = jnp.zeros_like(acc_ref)
        p_ref[...] = jnp.zeros_like(p_ref)
        store_scores(score_block(0))
        maxima = tuple(jnp.full((1, tq), NEG, F32) for _ in streams)
        alphas = tuple(jnp.ones((1, tq), F32) for _ in streams)
        maxima, alphas = lax.fori_loop(0, n_full, step, (maxima, alphas))
        done = value_block(jnp.maximum(n_full - 1, 0), alphas)
        for e, mask in enumerate(edge_masks):
            if e > 0:
                store_scores(score_block(n_full + e), [after(m) for m in maxima])
            maxima, alphas = max_block(maxima, mask, done)
            done = value_block(n_full + e, alphas)

    for c in range(2):
        a1, a2 = acc_ref[2 * c], acc_ref[2 * c + 1]
        o = a1[:VD_A] / a1[VD_A:VD_A + 1] - lam * (a2[:VD_A] / a2[VD_A:VD_A + 1])
        o = o * lax.rsqrt(jnp.mean(o * o, axis=0, keepdims=True) + EPS)
        o = o * g_ref[...] * (1.0 - LAM_INIT)
        o_ref[:, c * VD_A:(c + 1) * VD_A] = o.T.astype(BF16)


def _diff_prompt(qa_t, ka_t, va_tt, lam_vecs, g_head, batch, seq, tk):
    tq = 2 * tk if seq % (2 * tk) == 0 else tk
    nq = seq // tq
    half = N_GROUPS // 2
    q_spec = lambda off: pl.BlockSpec((1, tq, LANES), lambda b, p, i: (p + off, b * nq + i, 0))
    k_spec = lambda off: pl.BlockSpec((1, seq, LANES), lambda b, p, i: (p + off, b, 0))
    return pl.pallas_call(
        functools.partial(_diff_prompt_kernel, tq=tq, tk=tk, seq=seq),
        grid=(batch, half, nq),
        in_specs=[q_spec(0), q_spec(half), k_spec(0), k_spec(half),
                  pl.BlockSpec((2, seq // tk, VD_A + ONES_ROWS, tk), lambda b, p, i: (p, b, 0, 0)),
                  pl.BlockSpec((4, HD_A), lambda b, p, i: (0, 0)),
                  pl.BlockSpec((VD_A, 1), lambda b, p, i: (0, 0))],
        out_specs=pl.BlockSpec((tq, 2 * VD_A), lambda b, p, i: (b * nq + i, p)),
        out_shape=jax.ShapeDtypeStruct((batch * seq, SEG_W), BF16),
        scratch_shapes=[pltpu.VMEM((4, VD_A + ONES_ROWS, tq), F32), pltpu.VMEM((4, tk, tq), F32),
                        pltpu.VMEM((2, 4, tk, tq), BF16), pltpu.VMEM((4, 1, tq), F32)],
        compiler_params=_params("arbitrary", "arbitrary", "arbitrary"),
        name="diff_prompt",
    )(qa_t, qa_t, ka_t, ka_t, va_tt, lam_vecs, g_head.reshape(VD_A, 1))


def _stick_blocks(zs, mask, carries, vs, acc_ref, suffix_mat):
    terms = [_log_keep_terms(z, mask) for z in zs]
    suffixes = [_dot(t, suffix_mat) for t in terms]
    weights = []
    for z, suffix, carry in zip(zs, suffixes, carries):
        a = jnp.exp(z + suffix + carry)
        if mask is not None:
            a = jnp.where(mask, a, 0.0)
        weights.append(a.astype(BF16))
    for idx, (a, v) in enumerate(zip(weights, vs)):
        acc_ref[idx] += _dot(a, v)
    return tuple(carry + suffix[:, 0:1] for carry, suffix in zip(carries, suffixes))


def _stick_prompt_kernel(q_ref, k_ref, v_ref, o_ref, acc_ref, *, tq, tk, heads):
    qi = pl.program_id(2)
    qs = [q_ref[h] for h in range(heads)]
    suffix_mat = _suffix_matrix(tk)
    scale = HD_B ** -0.5
    acc_ref[...] = jnp.zeros_like(acc_ref)
    q_pos = qi * tq + lax.broadcasted_iota(jnp.int32, (tq, 1), 0)

    def block(start, mask, carries):
        zs = [_dot_nt(qs[h], k_ref[h, pl.ds(start, tk), :]) * scale for h in range(heads)]
        vs = [v_ref[h, pl.ds(start, tk), :] for h in range(heads)]
        return _stick_blocks(zs, mask, carries, vs, acc_ref, suffix_mat)

    carries = tuple(jnp.zeros((tq, 1), F32) for _ in range(heads))
    for e in range(tq // tk - 1, -1, -1):
        start = pl.multiple_of(qi * tq + e * tk, tk)
        k_pos = start + lax.broadcasted_iota(jnp.int32, (1, tk), 1)
        carries = block(start, k_pos < q_pos, carries)

    n_full = (qi * tq) // tk

    def cond(st):
        i, carries = st
        largest = functools.reduce(jnp.maximum, [jnp.max(c) for c in carries])
        return jnp.logical_and(i < n_full, largest > UNDERFLOW_LOG)

    def body(st):
        i, carries = st
        return i + 1, block(pl.multiple_of((n_full - 1 - i) * tk, tk), None, carries)

    lax.while_loop(cond, body, (jnp.int32(0), carries))
    for h in range(heads):
        o_ref[:, h * HD_B:(h + 1) * HD_B] = acc_ref[h].astype(BF16)


def _stick_prompt(qb_t, kb_t, vb_t, batch, seq):
    tq = _tile(seq, 256, 8)
    tk = tq
    heads = 4
    nq = seq // tq
    kv_spec = pl.BlockSpec((heads, seq, LANES), lambda b, h, i: (h, b, 0), pipeline_mode=pl.Buffered(1))
    return pl.pallas_call(
        functools.partial(_stick_prompt_kernel, tq=tq, tk=tk, heads=heads),
        grid=(batch, HB // heads, nq),
        in_specs=[pl.BlockSpec((heads, tq, LANES), lambda b, h, i: (h, b * nq + i, 0)), kv_spec, kv_spec],
        out_specs=pl.BlockSpec((tq, heads * HD_B), lambda b, h, i: (b * nq + i, h)),
        out_shape=jax.ShapeDtypeStruct((batch * seq, SEG_W), BF16),
        scratch_shapes=[pltpu.VMEM((heads, tq, HD_B), F32)],
        compiler_params=_params("parallel", "parallel", "arbitrary"),
        name="stick_prompt",
    )(qb_t, kb_t, vb_t)


def _diff_sample_kernel(qa_ref, ka_ref, va_ref, kc_ref, vc_ref, lam_ref, g_ref, o_ref, *, ts, past):
    lam = _lambda(lam_ref)
    q_pos = past + lax.broadcasted_iota(jnp.int32, (ts, 1), 0)
    limit = (q_pos // CHUNK + 1) * CHUNK
    k_pos = past + lax.broadcasted_iota(jnp.int32, (1, ts), 1)
    new_mask = k_pos < limit

    def attend(g, c, v_cache, v_new):
        q = _head_lanes(qa_ref[g], c)
        k_cache = jnp.concatenate([kc_ref[0, 2 * g].astype(BF16), kc_ref[0, 2 * g + 1].astype(BF16)], axis=0)
        s_cache = _dot(q, k_cache)
        s_new = jnp.where(new_mask, _dot_nt(q, ka_ref[g]), NEG)
        m = jnp.maximum(jnp.max(s_cache, axis=-1, keepdims=True), jnp.max(s_new, axis=-1, keepdims=True))
        p_cache = jnp.exp(s_cache - m)
        p_new = jnp.exp(s_new - m)
        l = jnp.sum(p_cache, axis=-1, keepdims=True) + jnp.sum(p_new, axis=-1, keepdims=True)
        return (_dot(p_cache.astype(BF16), v_cache) + _dot(p_new.astype(BF16), v_new)) / l

    half = N_GROUPS // 2
    for h in range(HA):
        g, c = h // 2, h % 2
        v_cache = vc_ref[0, :, h, :].astype(BF16)
        o1 = attend(g, c, v_cache, va_ref[h])
        o2 = attend(g + half, c, v_cache, va_ref[h])
        o_ref[:, h * VD_A:(h + 1) * VD_A] = _diff_finish(o1, o2, lam, g_ref[...])


def _diff_sample(qa_t, ka_t, va_t, cache_k, cache_v, lam_vecs, g_head, batch, ts, past):
    new_spec = pl.BlockSpec((N_GROUPS, ts, LANES), lambda b: (0, b, 0))
    return pl.pallas_call(
        functools.partial(_diff_sample_kernel, ts=ts, past=past),
        grid=(batch,),
        in_specs=[new_spec, new_spec, new_spec,
                  pl.BlockSpec((1, 2 * HA, HD_A, past), lambda b: (b, 0, 0, 0)),
                  pl.BlockSpec((1, past, HA, VD_A), lambda b: (b, 0, 0, 0)),
                  pl.BlockSpec((4, HD_A), lambda b: (0, 0)),
                  pl.BlockSpec((1, VD_A), lambda b: (0, 0))],
        out_specs=pl.BlockSpec((ts, SEG_W), lambda b: (b, 0)),
        out_shape=jax.ShapeDtypeStruct((batch * ts, SEG_W), BF16),
        compiler_params=_params("parallel"),
        name="diff_sample",
    )(qa_t, ka_t, va_t, cache_k, cache_v, lam_vecs, g_head)


def _stick_sample_kernel(q_ref, kn_ref, vn_ref, kc_ref, vc_ref, o_ref, acc_ref, *, ts, past, tk):
    scale = HD_B ** -0.5
    row = lax.broadcasted_iota(jnp.int32, (ts, 1), 0)
    col = lax.broadcasted_iota(jnp.int32, (1, ts), 1)
    new_suffix_mat = _suffix_matrix(ts)
    suffix_mat = _suffix_matrix(tk)
    n_full = past // tk

    for h in range(HB):
        q = q_ref[h]
        acc_ref[...] = jnp.zeros_like(acc_ref)
        z = _dot_nt(q, kn_ref[h]) * scale
        carry = _stick_block(z, col < row, jnp.zeros((ts, 1), F32), vn_ref[h], acc_ref, new_suffix_mat)

        def cond(st):
            i, carry = st
            return jnp.logical_and(i < n_full, jnp.max(carry) > UNDERFLOW_LOG)

        def body(st, q=q, h=h):
            i, carry = st
            start = pl.multiple_of((n_full - 1 - i) * tk, tk)
            k = kc_ref[0, pl.ds(start, tk), h, :].astype(BF16)
            v = vc_ref[0, pl.ds(start, tk), h, :].astype(BF16)
            return i + 1, _stick_block(_dot_nt(q, k) * scale, None, carry, v, acc_ref, suffix_mat)

        lax.while_loop(cond, body, (jnp.int32(0), carry))
        o_ref[:, h * HD_B:(h + 1) * HD_B] = acc_ref[...].astype(BF16)


def _stick_sample(qb_t, kb_t, vb_t, cache_k, cache_v, batch, ts, past):
    tk = _tile(past, 256, 8)
    new_spec = pl.BlockSpec((N_GROUPS, ts, LANES), lambda b: (0, b, 0))
    cache_spec = pl.BlockSpec((1, past, HB, HD_B), lambda b: (b, 0, 0, 0))
    return pl.pallas_call(
        functools.partial(_stick_sample_kernel, ts=ts, past=past, tk=tk),
        grid=(batch,),
        in_specs=[new_spec, new_spec, new_spec, cache_spec, cache_spec],
        out_specs=pl.BlockSpec((ts, SEG_W), lambda b: (b, 0)),
        out_shape=jax.ShapeDtypeStruct((batch * ts, SEG_W), BF16),
        scratch_shapes=[pltpu.VMEM((ts, HD_B), F32)],
        compiler_params=_params("parallel"),
        name="stick_sample",
    )(qb_t, kb_t, vb_t, cache_k, cache_v)


def _merge_kernel(h_ref, oa_ref, ob_ref, wga_ref, wgb_ref, wpa_ref, wpb_ref, o_ref):
    h = h_ref[...]
    ga = jax.nn.sigmoid(_dot(h, wga_ref[...]))
    gb = jax.nn.sigmoid(_dot(h, wgb_ref[...]))
    ya = _dot(oa_ref[...], wpa_ref[...])
    yb = _dot(ob_ref[...], wpb_ref[...])
    o_ref[...] = (ga * ya + gb * yb).astype(BF16)


def _merge(h, oa, ob, w_gate, w_proj_a, w_proj_b):
    rows, d = h.shape
    tm = _tile(rows, 512)
    tn = 512
    nn = d // tn
    row_spec = lambda w: pl.BlockSpec((tm, w), lambda i, n: (i, 0))
    return pl.pallas_call(
        _merge_kernel,
        grid=(rows // tm, nn),
        in_specs=[row_spec(d), row_spec(SEG_W), row_spec(SEG_W),
                  pl.BlockSpec((d, tn), lambda i, n: (0, n)),
                  pl.BlockSpec((d, tn), lambda i, n: (0, nn + n)),
                  pl.BlockSpec((SEG_W, tn), lambda i, n: (0, n)),
                  pl.BlockSpec((SEG_W, tn), lambda i, n: (0, n))],
        out_specs=pl.BlockSpec((tm, tn), lambda i, n: (i, n)),
        out_shape=jax.ShapeDtypeStruct((rows, d), BF16),
        compiler_params=_params("parallel", "arbitrary"),
        name="merge",
    )(h, oa, ob, w_gate, w_gate, w_proj_a, w_proj_b)


def _out_kernel(x_ref, m_ref, w_ref, g_post_ref, g_pre_ref, x1_ref, hm_ref):
    x1 = x_ref[...] + _rms(_dot(m_ref[...], w_ref[...]), g_post_ref[...])
    x1_ref[...] = x1
    hm_ref[...] = _rms(x1, g_pre_ref[...]).astype(BF16)


def _out_proj(x, merged, w_out, g_post_mix, g_pre_mlp):
    rows, d = x.shape
    tm = _tile(rows, 256)
    row = pl.BlockSpec((tm, d), lambda i: (i, 0))
    vec = pl.BlockSpec((1, d), lambda i: (0, 0))
    return pl.pallas_call(
        _out_kernel,
        grid=(rows // tm,),
        in_specs=[row, row, pl.BlockSpec((d, d), lambda i: (0, 0)), vec, vec],
        out_specs=[row, row],
        out_shape=[jax.ShapeDtypeStruct((rows, d), F32), jax.ShapeDtypeStruct((rows, d), BF16)],
        compiler_params=_params("parallel"),
        name="out_proj",
    )(x, merged, w_out, g_post_mix, g_pre_mlp)


def _mlp_kernel(x1_ref, hm_ref, wu_ref, wd_ref, g_ref, y_ref, acc_ref):
    f = pl.program_id(1)

    @pl.when(f == 0)
    def _():
        acc_ref[...] = jnp.zeros_like(acc_ref)

    hm = hm_ref[...]
    half = wu_ref.shape[1] // 2
    down = []
    for s in range(2):
        u = jnp.square(jnp.maximum(_dot(hm, wu_ref[:, s * half:(s + 1) * half]), 0.0))
        down.append(_dot(u.astype(BF16), wd_ref[s * half:(s + 1) * half, :]))
    acc_ref[...] += down[0] + down[1]

    @pl.when(f == pl.num_programs(1) - 1)
    def _():
        y_ref[...] = x1_ref[...] + _rms(acc_ref[...], g_ref[...])


def _mlp(x1, hm, w_up, w_down, g_post_mlp):
    rows, d = x1.shape
    d_ff = w_up.shape[1]
    tm = _tile(rows, 512)
    tf = 1024
    row = pl.BlockSpec((tm, d), lambda i, f: (i, 0))
    return pl.pallas_call(
        _mlp_kernel,
        grid=(rows // tm, d_ff // tf),
        in_specs=[row, row,
                  pl.BlockSpec((d, tf), lambda i, f: (0, f)),
                  pl.BlockSpec((tf, d), lambda i, f: (f, 0)),
                  pl.BlockSpec((1, d), lambda i, f: (0, 0))],
        out_specs=row,
        out_shape=jax.ShapeDtypeStruct((rows, d), F32),
        scratch_shapes=[pltpu.VMEM((tm, d), F32)],
        compiler_params=_params("parallel", "arbitrary"),
        name="mlp",
    )(x1, hm, w_up, w_down, g_post_mlp)


def _project_all(h, w_segs, cos, sin, v_block=None):
    _, qa_t = _project(h, w_segs[0], "q_rope", cos, sin)
    if v_block is None:
        ka_f, ka_t = _project(h, w_segs[1], "k_rope", cos, sin)
    else:
        ka_f, ka_t = _project(h, w_segs[1], "k_rope_t", cos, sin, tk=cos.shape[0])
    if v_block is None:
        va_f, va_t = _project(h, w_segs[2], "kv")
    else:
        va_f, va_t = _project(h, w_segs[2], "v_blocks", tk=v_block)
    _, qb_t = _project(h, w_segs[3], "q")
    kb_f, kb_t = _project(h, w_segs[4], "kv")
    vb_f, vb_t = _project(h, w_segs[5], "kv")
    return (qa_t, ka_t, va_t, qb_t, kb_t, vb_t), (ka_f, va_f, kb_f, vb_f)


def _tail(x, h, oa, ob, w):
    merged = _merge(h, oa, ob, w["gate"], w["proj_a"], w["proj_b"])
    x1, hm = _out_proj(x, merged, w["out"], w["g_post_mix"], w["g_pre_mlp"])
    return _mlp(x1, hm, w["up"], w["down"], w["g_post_mlp"])


def kernel(x_prompt, x_sample, cache_diff_k, cache_diff_v, cache_sb_k, cache_sb_v, g_pre_mix, w_in, lambda_q1, lambda_k1, lambda_q2, lambda_k2, g_diff_head, w_gate, w_proj_a, w_proj_b, w_out, g_post_mix, g_pre_mlp, w_up, w_down, g_post_mlp):
    bp, tp, d = x_prompt.shape
    bs, ts, _ = x_sample.shape
    depth, _, past = cache_diff_k.shape[:3]
    assert depth == 1 and tp % CHUNK == 0

    w_in_b = w_in[0].astype(BF16)
    w_segs = [w_in_b[:, s * SEG_W:(s + 1) * SEG_W] for s in range(6)]
    w = dict(gate=w_gate[0].astype(BF16), proj_a=w_proj_a[0].astype(BF16), proj_b=w_proj_b[0].astype(BF16),
             out=w_out[0].astype(BF16), up=w_up[0].astype(BF16), down=w_down[0].astype(BF16),
             g_post_mix=g_post_mix, g_pre_mlp=g_pre_mlp, g_post_mlp=g_post_mlp)
    lam_vecs = jnp.concatenate([lambda_q1, lambda_k1, lambda_q2, lambda_k2], axis=0).astype(F32)

    xp = x_prompt.reshape(bp * tp, d)
    hp = _prenorm(xp, g_pre_mix)
    cos_p, sin_p = _rope_tables(jnp.arange(tp, dtype=jnp.int32))
    tk_diff = _tile(tp, 256, CHUNK)
    (qa_t, ka_t, va_tt, qb_t, kb_t, vb_t), kv_p = _project_all(hp, w_segs, cos_p, sin_p, v_block=tk_diff)
    oa = _diff_prompt(qa_t, ka_t, va_tt, lam_vecs, g_diff_head, bp, tp, tk_diff)
    ob = _stick_prompt(qb_t, kb_t, vb_t, bp, tp)
    yp = _tail(xp, hp, oa, ob, w).reshape(bp, tp, d)

    xs = x_sample.reshape(bs * ts, d)
    hs = _prenorm(xs, g_pre_mix)
    pos_s = past + jnp.arange(ts, dtype=jnp.int32)
    cos_s, sin_s = _rope_tables(jnp.tile(pos_s, bs))
    (qa_t, ka_t, va_t, qb_t, kb_t, vb_t), kv_s = _project_all(hs, w_segs, cos_s, sin_s)
    oa = _diff_sample(qa_t, ka_t, va_t, jnp.transpose(cache_diff_k[0], (0, 2, 3, 1)), cache_diff_v[0],
                      lam_vecs, g_diff_head, bs, ts, past)
    ob = _stick_sample(qb_t, kb_t, vb_t, cache_sb_k[0], cache_sb_v[0], bs, ts, past)
    ys = _tail(xs, hs, oa, ob, w).reshape(bs, ts, d)

    def caches(kv, b, t):
        ka_f, va_f, kb_f, vb_f = kv
        if ka_f.ndim == 4:
            ka_f = jnp.transpose(ka_f, (0, 3, 1, 2))
        return (ka_f.reshape(1, b, t, 2 * HA, HD_A), va_f.reshape(1, b, t, HA, VD_A),
                kb_f.reshape(1, b, t, HB, HD_B), vb_f.reshape(1, b, t, HB, HD_B))

    return (yp, ys) + caches(kv_p, bp, tp) + caches(kv_s, bs, ts)
```

```python
import functools
import math

import jax
import jax.numpy as jnp
from jax import lax
from jax.experimental import pallas as pl
from jax.experimental.pallas import tpu as pltpu

F32 = jnp.float32
BF16 = jnp.bfloat16

CHUNK = 64
HA = 8
HD_A = 64
VD_A = 128
HB = 8
HD_B = 128
SEG_W = 1024
LANES = 128
N_GROUPS = SEG_W // LANES
ROPE_THETA = 10000.0
EPS = 1e-6
NEG = -1e30
LAM_INIT = 0.8 - 0.6 * math.exp(-0.3 * 0)
UNDERFLOW_LOG = -104.0
DENOM_MIN = 1e-20
TRIP_BLOCKS = (16, 8, 4, 2)
ONES_ROWS = 16
VMEM_LIMIT = 56 * 1024 * 1024


def _tile(n, pref, mult=8):
    if n <= pref:
        return n
    for t in range(pref, 0, -1):
        if n % t == 0 and t % mult == 0:
            return t
    raise ValueError(f"no tile for {n}")


def _params(*sem):
    return pltpu.CompilerParams(dimension_semantics=sem, vmem_limit_bytes=VMEM_LIMIT)


def _dot(a, b):
    return jnp.dot(a, b, preferred_element_type=F32)


def _dot_nt(a, b):
    return lax.dot_general(a, b, (((1,), (1,)), ((), ())), preferred_element_type=F32)


def _rms(x, g):
    return x * lax.rsqrt(jnp.mean(x * x, axis=-1, keepdims=True) + EPS) * g


def _prenorm_kernel(x_ref, g_ref, h_ref):
    h_ref[...] = _rms(x_ref[...], g_ref[...]).astype(BF16)


def _prenorm(x, g):
    rows, d = x.shape
    tm = _tile(rows, 512)
    return pl.pallas_call(
        _prenorm_kernel,
        grid=(rows // tm,),
        in_specs=[pl.BlockSpec((tm, d), lambda i: (i, 0)), pl.BlockSpec((1, d), lambda i: (0, 0))],
        out_specs=pl.BlockSpec((tm, d), lambda i: (i, 0)),
        out_shape=jax.ShapeDtypeStruct((rows, d), BF16),
        compiler_params=_params("parallel"),
        name="prenorm",
    )(x, g)


PROJ_CHUNK = 2


def _project_chunks(h_ref, w_ref):
    h = h_ref[...]
    for c0 in range(0, N_GROUPS, PROJ_CHUNK):
        yield c0, _dot(h, w_ref[:, c0 * LANES:(c0 + PROJ_CHUNK) * LANES])


def _groups(z):
    return [z[:, c * LANES:(c + 1) * LANES] for c in range(z.shape[1] // LANES)]


def _store_groups(t_ref, z, c0):
    for c, zc in enumerate(_groups(z)):
        t_ref[c0 + c] = zc.astype(BF16)


def _rope_groups(z, cos, sin):
    lane = lax.broadcasted_iota(jnp.int32, (1, LANES), 1)
    first_half = (lane % HD_A) < (HD_A // 2)
    out = []
    for zc in _groups(z):
        partner = jnp.where(first_half, pltpu.roll(zc, LANES - HD_A // 2, 1), pltpu.roll(zc, HD_A // 2, 1))
        out.append(zc * cos + partner * sin)
    return out


def _store_heads(f_ref, z, c0):
    for c, zc in enumerate(_groups(z)):
        f_ref[:, c0 + c, :] = zc


def _proj_kv_kernel(h_ref, w_ref, f_ref, t_ref):
    for c0, z in _project_chunks(h_ref, w_ref):
        _store_heads(f_ref, z, c0)
        _store_groups(t_ref, z, c0)


def _proj_v_blocks_kernel(h_ref, w_ref, f_ref, t_ref, *, tk):
    for c0, z in _project_chunks(h_ref, w_ref):
        _store_heads(f_ref, z, c0)
        for c, zc in enumerate(_groups(z)):
            zt = zc.T
            for j in range(zt.shape[1] // tk):
                t_ref[c0 + c, j, :LANES, :] = zt[:, j * tk:(j + 1) * tk].astype(BF16)
    row = lax.broadcasted_iota(jnp.int32, (ONES_ROWS, tk), 0)
    ones_rows = jnp.where(row == 0, 1.0, 0.0).astype(BF16)
    for c in range(N_GROUPS):
        for j in range(t_ref.shape[1]):
            t_ref[c, j, LANES:, :] = ones_rows


def _proj_q_kernel(h_ref, w_ref, t_ref):
    for c0, z in _project_chunks(h_ref, w_ref):
        _store_groups(t_ref, z, c0)


def _proj_k_rope_kernel(h_ref, w_ref, cos_ref, sin_ref, f_ref, t_ref):
    for c0, z in _project_chunks(h_ref, w_ref):
        for c, zc in enumerate(_rope_groups(z, cos_ref[...], sin_ref[...]), c0):
            f_ref[:, c * LANES:(c + 1) * LANES] = zc
            t_ref[c] = zc.astype(BF16)


def _proj_k_rope_t_kernel(h_ref, w_ref, cos_ref, sin_ref, f_ref, t_ref):
    for c0, z in _project_chunks(h_ref, w_ref):
        for c, zc in enumerate(_rope_groups(z, cos_ref[...], sin_ref[...]), c0):
            zt = zc.T
            f_ref[0, 2 * c] = zt[:HD_A]
            f_ref[0, 2 * c + 1] = zt[HD_A:]
            t_ref[c] = zc.astype(BF16)


def _proj_q_rope_kernel(h_ref, w_ref, cos_ref, sin_ref, t_ref):
    for c0, z in _project_chunks(h_ref, w_ref):
        for c, zc in enumerate(_rope_groups(z, cos_ref[...], sin_ref[...]), c0):
            t_ref[c] = (zc * (HD_A ** -0.5)).astype(BF16)


def _project(h, w, mode, cos=None, sin=None, tk=None):
    rows, d = h.shape
    tm = _tile(rows, 1024)
    n_tab = None if cos is None else cos.shape[0] // tm
    in_specs = [pl.BlockSpec((tm, d), lambda i: (i, 0)), pl.BlockSpec((d, SEG_W), lambda i: (0, 0))]
    args = [h, w]
    if cos is not None:
        tab = pl.BlockSpec((tm, LANES), lambda i: (i % n_tab, 0))
        in_specs += [tab, tab]
        args += [cos, sin]
    t_spec = pl.BlockSpec((N_GROUPS, tm, LANES), lambda i: (0, i, 0))
    t_shape = jax.ShapeDtypeStruct((N_GROUPS, rows, LANES), BF16)
    with_f32 = mode in ("kv", "k_rope", "k_rope_t", "v_blocks")
    body = {"kv": _proj_kv_kernel, "q": _proj_q_kernel, "k_rope": _proj_k_rope_kernel,
            "k_rope_t": _proj_k_rope_t_kernel, "q_rope": _proj_q_rope_kernel,
            "v_blocks": functools.partial(_proj_v_blocks_kernel, tk=tk)}[mode]
    if mode == "k_rope":
        f_spec = pl.BlockSpec((tm, SEG_W), lambda i: (i, 0))
        f_shape = jax.ShapeDtypeStruct((rows, SEG_W), F32)
    elif mode == "k_rope_t":
        f_spec = pl.BlockSpec((1, 2 * HA, HD_A, tm), lambda i: (i // n_tab, 0, 0, i % n_tab))
        f_shape = jax.ShapeDtypeStruct((rows // tk, 2 * HA, HD_A, tk), F32)
    else:
        f_spec = pl.BlockSpec((tm, N_GROUPS, LANES), lambda i: (i, 0, 0))
        f_shape = jax.ShapeDtypeStruct((rows, N_GROUPS, LANES), F32)
    if mode == "v_blocks":
        t_spec = pl.BlockSpec((N_GROUPS, tm // tk, LANES + ONES_ROWS, tk), lambda i: (0, i, 0, 0))
        t_shape = jax.ShapeDtypeStruct((N_GROUPS, rows // tk, LANES + ONES_ROWS, tk), BF16)
    out = pl.pallas_call(
        body,
        grid=(rows // tm,),
        in_specs=in_specs,
        out_specs=[f_spec, t_spec] if with_f32 else t_spec,
        out_shape=[f_shape, t_shape] if with_f32 else t_shape,
        compiler_params=_params("parallel"),
        name="proj_" + mode,
    )(*args)
    return (out[0], out[1]) if with_f32 else (None, out)


def _rope_tables(pos):
    half = HD_A // 2
    inv = ROPE_THETA ** (-jnp.arange(half, dtype=F32) / half)
    ang = pos.astype(F32)[:, None] * inv[None, :]
    cos, sin = jnp.cos(ang), jnp.sin(ang)
    reps = LANES // HD_A
    return (jnp.tile(jnp.concatenate([cos, cos], axis=-1), (1, reps)),
            jnp.tile(jnp.concatenate([-sin, sin], axis=-1), (1, reps)))


def _head_lanes(q, c):
    lane = lax.broadcasted_iota(jnp.int32, (1, LANES), 1)
    return jnp.where((lane >= c * HD_A) & (lane < (c + 1) * HD_A), q, jnp.zeros_like(q))


def _suffix_matrix(n):
    j = lax.broadcasted_iota(jnp.int32, (2 * n, n), 0) % n
    s = lax.broadcasted_iota(jnp.int32, (2 * n, n), 1)
    return jnp.where(j >= s, 1.0, 0.0).astype(BF16)


def _log_keep_terms(z, mask):
    neg = -z
    log_keep = jnp.minimum(neg, 0.0) - jnp.log(1.0 + jnp.exp(jnp.minimum(z, neg)))
    if mask is not None:
        log_keep = jnp.where(mask, log_keep, 0.0)
    hi = log_keep.astype(BF16)
    return jnp.concatenate([hi, (log_keep - hi.astype(F32)).astype(BF16)], axis=1)


def _stick_block(z, mask, carry, v, acc_ref, suffix_mat):
    suffix = _dot(_log_keep_terms(z, mask), suffix_mat)
    a = jnp.exp(z + suffix + carry)
    if mask is not None:
        a = jnp.where(mask, a, 0.0)
    acc_ref[...] += _dot(a.astype(BF16), v)
    return carry + suffix[:, 0:1]


def _lambda(lam_ref):
    lam = lam_ref[...]
    d1 = jnp.sum(lam[0:1] * lam[1:2], axis=-1, keepdims=True)
    d2 = jnp.sum(lam[2:3] * lam[3:4], axis=-1, keepdims=True)
    return jnp.exp(d1) - jnp.exp(d2) + LAM_INIT


def _diff_finish(o1, o2, lam, g_head):
    o = o1 - lam * o2
    o = o * lax.rsqrt(jnp.mean(o * o, axis=-1, keepdims=True) + EPS)
    return (o * g_head * (1.0 - LAM_INIT)).astype(BF16)


def _key_norm_bound(k_ref, c, seq, tq):
    rows = _tile(seq, 2048, 16)
    lane = lax.broadcasted_iota(jnp.int32, (8, LANES), 1)
    ones_c = jnp.where((lane >= c * HD_A) & (lane < (c + 1) * HD_A), 1.0, 0.0).astype(BF16)

    def body(i, best):
        k = k_ref[0, pl.ds(pl.multiple_of(i * rows, rows), rows), :].astype(F32)
        return jnp.maximum(best, _dot_nt(ones_c, (k * k).astype(BF16)))

    best = lax.fori_loop(0, seq // rows, body, jnp.zeros((8, rows), F32))
    return jnp.broadcast_to(jnp.max(best[0:1], axis=1, keepdims=True), (1, tq))


def _diff_prompt_kernel(q1_ref, q2_ref, k1_ref, k2_ref, vt_ref, lam_ref, g_ref, o_ref, acc_ref, s_ref, p_ref,
                        kmax_ref, *, tq, tk, seq):
    qi = pl.program_id(2)
    lam = _lambda(lam_ref)

    @pl.when(qi == 0)
    def _():
        for idx, (c, k_ref) in enumerate((c, k_ref) for c in range(2) for k_ref in (k1_ref, k2_ref)):
            kmax_ref[idx] = _key_norm_bound(k_ref, c, seq, tq)

    n_full = (qi * tq) // tk
    n_edge = tq // tk
    q_pos = qi * tq + lax.broadcasted_iota(jnp.int32, (1, tq), 1)
    limit = (q_pos // CHUNK + 1) * CHUNK
    k_pos = n_full * tk + lax.broadcasted_iota(jnp.int32, (tk, 1), 0)
    edge_masks = [k_pos + e * tk < limit for e in range(n_edge)]

    streams = [(c, _head_lanes(q_ref[0], c), k_ref)
               for c in range(2) for q_ref, k_ref in ((q1_ref, k1_ref), (q2_ref, k2_ref))]
    n_streams = len(streams)

    def score_block(j):
        start = pl.multiple_of(j * tk, tk)
        return [_dot_nt(k_ref[0, pl.ds(start, tk), :], q) for _, q, k_ref in streams]

    def after(x):
        return 0.0 * x

    def store_scores(scores, readers=None):
        for idx, s in enumerate(scores):
            s_ref[idx] = s if readers is None else s + readers[idx]

    def value_block(j, alphas, slot=1):
        done = []
        for idx, (c, _, _) in enumerate(streams):
            pv = _dot(vt_ref[c, j], p_ref[slot, idx])
            acc_ref[idx] = pv + (acc_ref[idx] if alphas is None else alphas[idx] * acc_ref[idx])
            done.append(after(pv[VD_A:VD_A + 1, :]))
        return done

    def masked(s, mask):
        return s if mask is None else jnp.where(mask, s, NEG)

    ones8 = jnp.ones((8, LANES), BF16)
    bounds = []
    for idx, (_, q, _) in enumerate(streams):
        qf = q.astype(F32)
        q_sq = _dot_nt(ones8, (qf * qf).astype(BF16))[0:1]
        bounds.append(jnp.sqrt(q_sq * kmax_ref[idx]) * 1.02)

    def store_weights(scores, mask, slot, readers=None):
        for idx, s in enumerate(scores):
            shift = bounds[idx] if readers is None else bounds[idx] + readers[idx]
            p_ref[slot, idx] = jnp.exp(masked(s, mask) - shift).astype(BF16)

    def bound_block(j, mask):
        scores = score_block(j)
        done = value_block(jnp.maximum(j - 1, 0), None)
        store_weights(scores, mask, 1, done)

    def bound_blocks(j, masks):
        n = len(masks)
        scores = score_block(j)
        done = {1: value_block(jnp.maximum(j - 1, 0), None), 0: None}
        for u, mask in enumerate(masks):
            slot = u % 2
            next_scores = score_block(j + u + 1) if u < n - 1 else None
            store_weights(scores, mask, slot, done[slot])
            if u < n - 1:
                done[slot] = value_block(j + u, None, slot=slot)
            scores = next_scores

    acc_ref[...] = jnp.zeros_like(acc_ref)
    p_ref[1] = jnp.zeros_like(p_ref[1])
    first = 0
    for n in TRIP_BLOCKS:
        trips = (n_full - first) // n
        lax.fori_loop(first // n, first // n + trips, lambda i, _, n=n: bound_blocks(n * i, [None] * n), None)
        first = first + trips * n

    @pl.when(n_full % 2 == 1)
    def _():
        bound_block(n_full - 1, None)

    if n_edge % 2 == 0:
        bound_blocks(n_full, edge_masks)
    else:
        for e, mask in enumerate(edge_masks):
            bound_block(n_full + e, mask)
    value_block(n_full + n_edge - 1, None)

    denom_min = acc_ref[0, VD_A:VD_A + 1, :]
    for idx in range(1, n_streams):
        denom_min = jnp.minimum(denom_min, acc_ref[idx, VD_A:VD_A + 1, :])

    @pl.when(jnp.logical_not(jnp.min(denom_min) >= DENOM_MIN))
    def _():
        def max_block(maxima, mask, done):
            new_maxima, alphas = [], []
            for idx in range(n_streams):
                s = masked(s_ref[idx], mask)
                m_new = jnp.maximum(maxima[idx], jnp.max(s, axis=0, keepdims=True))
                alphas.append(jnp.exp(maxima[idx] - m_new))
                new_maxima.append(m_new)
                p_ref[1, idx] = jnp.exp(s - (m_new + done[idx])).astype(BF16)
            return tuple(new_maxima), tuple(alphas)

        def step(j, carry):
            maxima, alphas = carry
            next_scores = score_block(j + 1)
            done = value_block(jnp.maximum(j - 1, 0), alphas)
            maxima, alphas = max_block(maxima, None, done)
            store_scores(next_scores, [after(m) for m in maxima])
            return maxima, alphas

        acc_ref[...] = jnp.zeros_like(acc_ref)
        p_ref[...] = jnp.zeros_like(p_ref)
        store_scores(score_block(0))
        maxima = tuple(jnp.full((1, tq), NEG, F32) for _ in streams)
        alphas = tuple(jnp.ones((1, tq), F32) for _ in streams)
        maxima, alphas = lax.fori_loop(0, n_full, step, (maxima, alphas))
        done = value_block(jnp.maximum(n_full - 1, 0), alphas)
        for e, mask in enumerate(edge_masks):
            if e > 0:
                store_scores(score_block(n_full + e), [after(m) for m in maxima])
            maxima, alphas = max_block(maxima, mask, done)
            done = value_block(n_full + e, alphas)

    for c in range(2):
        a1, a2 = acc_ref[2 * c], acc_ref[2 * c + 1]
        o = a1[:VD_A] / a1[VD_A:VD_A + 1] - lam * (a2[:VD_A] / a2[VD_A:VD_A + 1])
        o = o * lax.rsqrt(jnp.mean(o * o, axis=0, keepdims=True) + EPS)
        o = o * g_ref[...] * (1.0 - LAM_INIT)
        o_ref[:, c * VD_A:(c + 1) * VD_A] = o.T.astype(BF16)


def _diff_prompt(qa_t, ka_t, va_tt, lam_vecs, g_head, batch, seq, tk):
    tq = 2 * tk if seq % (2 * tk) == 0 else tk
    nq = seq // tq
    half = N_GROUPS // 2
    q_spec = lambda off: pl.BlockSpec((1, tq, LANES), lambda b, p, i: (p + off, b * nq + i, 0))
    k_spec = lambda off: pl.BlockSpec((1, seq, LANES), lambda b, p, i: (p + off, b, 0))
    return pl.pallas_call(
        functools.partial(_diff_prompt_kernel, tq=tq, tk=tk, seq=seq),
        grid=(batch, half, nq),
        in_specs=[q_spec(0), q_spec(half), k_spec(0), k_spec(half),
                  pl.BlockSpec((2, seq // tk, VD_A + ONES_ROWS, tk), lambda b, p, i: (p, b, 0, 0)),
                  pl.BlockSpec((4, HD_A), lambda b, p, i: (0, 0)),
                  pl.BlockSpec((VD_A, 1), lambda b, p, i: (0, 0))],
        out_specs=pl.BlockSpec((tq, 2 * VD_A), lambda b, p, i: (b * nq + i, p)),
        out_shape=jax.ShapeDtypeStruct((batch * seq, SEG_W), BF16),
        scratch_shapes=[pltpu.VMEM((4, VD_A + ONES_ROWS, tq), F32), pltpu.VMEM((4, tk, tq), F32),
                        pltpu.VMEM((2, 4, tk, tq), BF16), pltpu.VMEM((4, 1, tq), F32)],
        compiler_params=_params("arbitrary", "arbitrary", "arbitrary"),
        name="diff_prompt",
    )(qa_t, qa_t, ka_t, ka_t, va_tt, lam_vecs, g_head.reshape(VD_A, 1))


def _stick_blocks(zs, mask, carries, vs, acc_ref, suffix_mat):
    terms = [_log_keep_terms(z, mask) for z in zs]
    suffixes = [_dot(t, suffix_mat) for t in terms]
    weights = []
    for z, suffix, carry in zip(zs, suffixes, carries):
        a = jnp.exp(z + suffix + carry)
        if mask is not None:
            a = jnp.where(mask, a, 0.0)
        weights.append(a.astype(BF16))
    for idx, (a, v) in enumerate(zip(weights, vs)):
        acc_ref[idx] += _dot(a, v)
    return tuple(carry + suffix[:, 0:1] for carry, suffix in zip(carries, suffixes))


def _stick_prompt_kernel(q_ref, k_ref, v_ref, o_ref, acc_ref, *, tq, tk, heads):
    qi = pl.program_id(2)
    qs = [q_ref[h] for h in range(heads)]
    suffix_mat = _suffix_matrix(tk)
    scale = HD_B ** -0.5
    acc_ref[...] = jnp.zeros_like(acc_ref)
    q_pos = qi * tq + lax.broadcasted_iota(jnp.int32, (tq, 1), 0)

    def block(start, mask, carries):
        zs = [_dot_nt(qs[h], k_ref[h, pl.ds(start, tk), :]) * scale for h in range(heads)]
        vs = [v_ref[h, pl.ds(start, tk), :] for h in range(heads)]
        return _stick_blocks(zs, mask, carries, vs, acc_ref, suffix_mat)

    carries = tuple(jnp.zeros((tq, 1), F32) for _ in range(heads))
    for e in range(tq // tk - 1, -1, -1):
        start = pl.multiple_of(qi * tq + e * tk, tk)
        k_pos = start + lax.broadcasted_iota(jnp.int32, (1, tk), 1)
        carries = block(start, k_pos < q_pos, carries)

    n_full = (qi * tq) // tk

    def cond(st):
        i, carries = st
        largest = functools.reduce(jnp.maximum, [jnp.max(c) for c in carries])
        return jnp.logical_and(i < n_full, largest > UNDERFLOW_LOG)

    def body(st):
        i, carries = st
        return i + 1, block(pl.multiple_of((n_full - 1 - i) * tk, tk), None, carries)

    lax.while_loop(cond, body, (jnp.int32(0), carries))
    for h in range(heads):
        o_ref[:, h * HD_B:(h + 1) * HD_B] = acc_ref[h].astype(BF16)


def _stick_prompt(qb_t, kb_t, vb_t, batch, seq):
    tq = _tile(seq, 256, 8)
    tk = tq
    heads = 4
    nq = seq // tq
    kv_spec = pl.BlockSpec((heads, seq, LANES), lambda b, h, i: (h, b, 0), pipeline_mode=pl.Buffered(1))
    return pl.pallas_call(
        functools.partial(_stick_prompt_kernel, tq=tq, tk=tk, heads=heads),
        grid=(batch, HB // heads, nq),
        in_specs=[pl.BlockSpec((heads, tq, LANES), lambda b, h, i: (h, b * nq + i, 0)), kv_spec, kv_spec],
        out_specs=pl.BlockSpec((tq, heads * HD_B), lambda b, h, i: (b * nq + i, h)),
        out_shape=jax.ShapeDtypeStruct((batch * seq, SEG_W), BF16),
        scratch_shapes=[pltpu.VMEM((heads, tq, HD_B), F32)],
        compiler_params=_params("parallel", "parallel", "arbitrary"),
        name="stick_prompt",
    )(qb_t, kb_t, vb_t)


def _diff_sample_kernel(qa_ref, ka_ref, va_ref, kc_ref, vc_ref, lam_ref, g_ref, o_ref, *, ts, past):
    lam = _lambda(lam_ref)
    q_pos = past + lax.broadcasted_iota(jnp.int32, (ts, 1), 0)
    limit = (q_pos // CHUNK + 1) * CHUNK
    k_pos = past + lax.broadcasted_iota(jnp.int32, (1, ts), 1)
    new_mask = k_pos < limit

    def attend(g, c, v_cache, v_new):
        q = _head_lanes(qa_ref[g], c)
        k_cache = jnp.concatenate([kc_ref[0, 2 * g].astype(BF16), kc_ref[0, 2 * g + 1].astype(BF16)], axis=0)
        s_cache = _dot(q, k_cache)
        s_new = jnp.where(new_mask, _dot_nt(q, ka_ref[g]), NEG)
        m = jnp.maximum(jnp.max(s_cache, axis=-1, keepdims=True), jnp.max(s_new, axis=-1, keepdims=True))
        p_cache = jnp.exp(s_cache - m)
        p_new = jnp.exp(s_new - m)
        l = jnp.sum(p_cache, axis=-1, keepdims=True) + jnp.sum(p_new, axis=-1, keepdims=True)
        return (_dot(p_cache.astype(BF16), v_cache) + _dot(p_new.astype(BF16), v_new)) / l

    half = N_GROUPS // 2
    for h in range(HA):
        g, c = h // 2, h % 2
        v_cache = vc_ref[0, :, h, :].astype(BF16)
        o1 = attend(g, c, v_cache, va_ref[h])
        o2 = attend(g + half, c, v_cache, va_ref[h])
        o_ref[:, h * VD_A:(h + 1) * VD_A] = _diff_finish(o1, o2, lam, g_ref[...])


def _diff_sample(qa_t, ka_t, va_t, cache_k, cache_v, lam_vecs, g_head, batch, ts, past):
    new_spec = pl.BlockSpec((N_GROUPS, ts, LANES), lambda b: (0, b, 0))
    return pl.pallas_call(
        functools.partial(_diff_sample_kernel, ts=ts, past=past),
        grid=(batch,),
        in_specs=[new_spec, new_spec, new_spec,
                  pl.BlockSpec((1, 2 * HA, HD_A, past), lambda b: (b, 0, 0, 0)),
                  pl.BlockSpec((1, past, HA, VD_A), lambda b: (b, 0, 0, 0)),
                  pl.BlockSpec((4, HD_A), lambda b: (0, 0)),
                  pl.BlockSpec((1, VD_A), lambda b: (0, 0))],
        out_specs=pl.BlockSpec((ts, SEG_W), lambda b: (b, 0)),
        out_shape=jax.ShapeDtypeStruct((batch * ts, SEG_W), BF16),
        compiler_params=_params("parallel"),
        name="diff_sample",
    )(qa_t, ka_t, va_t, cache_k, cache_v, lam_vecs, g_head)


def _stick_sample_kernel(q_ref, kn_ref, vn_ref, kc_ref, vc_ref, o_ref, acc_ref, *, ts, past, tk):
    scale = HD_B ** -0.5
    row = lax.broadcasted_iota(jnp.int32, (ts, 1), 0)
    col = lax.broadcasted_iota(jnp.int32, (1, ts), 1)
    new_suffix_mat = _suffix_matrix(ts)
    suffix_mat = _suffix_matrix(tk)
    n_full = past // tk

    for h in range(HB):
        q = q_ref[h]
        acc_ref[...] = jnp.zeros_like(acc_ref)
        z = _dot_nt(q, kn_ref[h]) * scale
        carry = _stick_block(z, col < row, jnp.zeros((ts, 1), F32), vn_ref[h], acc_ref, new_suffix_mat)

        def cond(st):
            i, carry = st
            return jnp.logical_and(i < n_full, jnp.max(carry) > UNDERFLOW_LOG)

        def body(st, q=q, h=h):
            i, carry = st
            start = pl.multiple_of((n_full - 1 - i) * tk, tk)
            k = kc_ref[0, pl.ds(start, tk), h, :].astype(BF16)
            v = vc_ref[0, pl.ds(start, tk), h, :].astype(BF16)
            return i + 1, _stick_block(_dot_nt(q, k) * scale, None, carry, v, acc_ref, suffix_mat)

        lax.while_loop(cond, body, (jnp.int32(0), carry))
        o_ref[:, h * HD_B:(h + 1) * HD_B] = acc_ref[...].astype(BF16)


def _stick_sample(qb_t, kb_t, vb_t, cache_k, cache_v, batch, ts, past):
    tk = _tile(past, 256, 8)
    new_spec = pl.BlockSpec((N_GROUPS, ts, LANES), lambda b: (0, b, 0))
    cache_spec = pl.BlockSpec((1, past, HB, HD_B), lambda b: (b, 0, 0, 0))
    return pl.pallas_call(
        functools.partial(_stick_sample_kernel, ts=ts, past=past, tk=tk),
        grid=(batch,),
        in_specs=[new_spec, new_spec, new_spec, cache_spec, cache_spec],
        out_specs=pl.BlockSpec((ts, SEG_W), lambda b: (b, 0)),
        out_shape=jax.ShapeDtypeStruct((batch * ts, SEG_W), BF16),
        scratch_shapes=[pltpu.VMEM((ts, HD_B), F32)],
        compiler_params=_params("parallel"),
        name="stick_sample",
    )(qb_t, kb_t, vb_t, cache_k, cache_v)


def _merge_kernel(h_ref, oa_ref, ob_ref, wga_ref, wgb_ref, wpa_ref, wpb_ref, o_ref):
    h = h_ref[...]
    ga = jax.nn.sigmoid(_dot(h, wga_ref[...]))
    gb = jax.nn.sigmoid(_dot(h, wgb_ref[...]))
    ya = _dot(oa_ref[...], wpa_ref[...])
    yb = _dot(ob_ref[...], wpb_ref[...])
    o_ref[...] = (ga * ya + gb * yb).astype(BF16)


def _merge(h, oa, ob, w_gate, w_proj_a, w_proj_b):
    rows, d = h.shape
    tm = _tile(rows, 1024)
    tn = 512
    nn = d // tn
    row_spec = lambda w: pl.BlockSpec((tm, w), lambda i, n: (i, 0))
    return pl.pallas_call(
        _merge_kernel,
        grid=(rows // tm, nn),
        in_specs=[row_spec(d), row_spec(SEG_W), row_spec(SEG_W),
                  pl.BlockSpec((d, tn), lambda i, n: (0, n)),
                  pl.BlockSpec((d, tn), lambda i, n: (0, nn + n)),
                  pl.BlockSpec((SEG_W, tn), lambda i, n: (0, n)),
                  pl.BlockSpec((SEG_W, tn), lambda i, n: (0, n))],
        out_specs=pl.BlockSpec((tm, tn), lambda i, n: (i, n)),
        out_shape=jax.ShapeDtypeStruct((rows, d), BF16),
        compiler_params=_params("parallel", "arbitrary"),
        name="merge",
    )(h, oa, ob, w_gate, w_gate, w_proj_a, w_proj_b)


def _out_kernel(x_ref, m_ref, w_ref, g_post_ref, g_pre_ref, x1_ref, hm_ref):
    x1 = x_ref[...] + _rms(_dot(m_ref[...], w_ref[...]), g_post_ref[...])
    x1_ref[...] = x1
    hm_ref[...] = _rms(x1, g_pre_ref[...]).astype(BF16)


def _out_proj(x, merged, w_out, g_post_mix, g_pre_mlp):
    rows, d = x.shape
    tm = _tile(rows, 512)
    row = pl.BlockSpec((tm, d), lambda i: (i, 0))
    vec = pl.BlockSpec((1, d), lambda i: (0, 0))
    return pl.pallas_call(
        _out_kernel,
        grid=(rows // tm,),
        in_specs=[row, row, pl.BlockSpec((d, d), lambda i: (0, 0)), vec, vec],
        out_specs=[row, row],
        out_shape=[jax.ShapeDtypeStruct((rows, d), F32), jax.ShapeDtypeStruct((rows, d), BF16)],
        compiler_params=_params("parallel"),
        name="out_proj",
    )(x, merged, w_out, g_post_mix, g_pre_mlp)


def _mlp_kernel(x1_ref, hm_ref, wu_ref, wd_ref, g_ref, y_ref, acc_ref):
    f = pl.program_id(1)

    @pl.when(f == 0)
    def _():
        acc_ref[...] = jnp.zeros_like(acc_ref)

    hm = hm_ref[...]
    half = wu_ref.shape[1] // 2
    down = []
    for s in range(2):
        u = jnp.square(jnp.maximum(_dot(hm, wu_ref[:, s * half:(s + 1) * half]), 0.0))
        down.append(_dot(u.astype(BF16), wd_ref[s * half:(s + 1) * half, :]))
    acc_ref[...] += down[0] + down[1]

    @pl.when(f == pl.num_programs(1) - 1)
    def _():
        y_ref[...] = x1_ref[...] + _rms(acc_ref[...], g_ref[...])


def _mlp(x1, hm, w_up, w_down, g_post_mlp):
    rows, d = x1.shape
    d_ff = w_up.shape[1]
    tm = _tile(rows, 512)
    tf = 1024
    row = pl.BlockSpec((tm, d), lambda i, f: (i, 0))
    return pl.pallas_call(
        _mlp_kernel,
        grid=(rows // tm, d_ff // tf),
        in_specs=[row, row,
                  pl.BlockSpec((d, tf), lambda i, f: (0, f)),
                  pl.BlockSpec((tf, d), lambda i, f: (f, 0)),
                  pl.BlockSpec((1, d), lambda i, f: (0, 0))],
        out_specs=row,
        out_shape=jax.ShapeDtypeStruct((rows, d), F32),
        scratch_shapes=[pltpu.VMEM((tm, d), F32)],
        compiler_params=_params("parallel", "arbitrary"),
        name="mlp",
    )(x1, hm, w_up, w_down, g_post_mlp)


def _project_all(h, w_segs, cos, sin, v_block=None):
    _, qa_t = _project(h, w_segs[0], "q_rope", cos, sin)
    if v_block is None:
        ka_f, ka_t = _project(h, w_segs[1], "k_rope", cos, sin)
    else:
        ka_f, ka_t = _project(h, w_segs[1], "k_rope_t", cos, sin, tk=cos.shape[0])
    if v_block is None:
        va_f, va_t = _project(h, w_segs[2], "kv")
    else:
        va_f, va_t = _project(h, w_segs[2], "v_blocks", tk=v_block)
    _, qb_t = _project(h, w_segs[3], "q")
    kb_f, kb_t = _project(h, w_segs[4], "kv")
    vb_f, vb_t = _project(h, w_segs[5], "kv")
    return (qa_t, ka_t, va_t, qb_t, kb_t, vb_t), (ka_f, va_f, kb_f, vb_f)


def _tail(x, h, oa, ob, w):
    merged = _merge(h, oa, ob, w["gate"], w["proj_a"], w["proj_b"])
    x1, hm = _out_proj(x, merged, w["out"], w["g_post_mix"], w["g_pre_mlp"])
    return _mlp(x1, hm, w["up"], w["down"], w["g_post_mlp"])


def kernel(x_prompt, x_sample, cache_diff_k, cache_diff_v, cache_sb_k, cache_sb_v, g_pre_mix, w_in, lambda_q1, lambda_k1, lambda_q2, lambda_k2, g_diff_head, w_gate, w_proj_a, w_proj_b, w_out, g_post_mix, g_pre_mlp, w_up, w_down, g_post_mlp):
    bp, tp, d = x_prompt.shape
    bs, ts, _ = x_sample.shape
    depth, _, past = cache_diff_k.shape[:3]
    assert depth == 1 and tp % CHUNK == 0

    w_in_b = w_in[0].astype(BF16)
    w_segs = [w_in_b[:, s * SEG_W:(s + 1) * SEG_W] for s in range(6)]
    w = dict(gate=w_gate[0].astype(BF16), proj_a=w_proj_a[0].astype(BF16), proj_b=w_proj_b[0].astype(BF16),
             out=w_out[0].astype(BF16), up=w_up[0].astype(BF16), down=w_down[0].astype(BF16),
             g_post_mix=g_post_mix, g_pre_mlp=g_pre_mlp, g_post_mlp=g_post_mlp)
    lam_vecs = jnp.concatenate([lambda_q1, lambda_k1, lambda_q2, lambda_k2], axis=0).astype(F32)

    xp = x_prompt.reshape(bp * tp, d)
    hp = _prenorm(xp, g_pre_mix)
    cos_p, sin_p = _rope_tables(jnp.arange(tp, dtype=jnp.int32))
    tk_diff = _tile(tp, 256, CHUNK)
    (qa_t, ka_t, va_tt, qb_t, kb_t, vb_t), kv_p = _project_all(hp, w_segs, cos_p, sin_p, v_block=tk_diff)
    oa = _diff_prompt(qa_t, ka_t, va_tt, lam_vecs, g_diff_head, bp, tp, tk_diff)
    ob = _stick_prompt(qb_t, kb_t, vb_t, bp, tp)
    yp = _tail(xp, hp, oa, ob, w).reshape(bp, tp, d)

    xs = x_sample.reshape(bs * ts, d)
    hs = _prenorm(xs, g_pre_mix)
    pos_s = past + jnp.arange(ts, dtype=jnp.int32)
    cos_s, sin_s = _rope_tables(jnp.tile(pos_s, bs))
    (qa_t, ka_t, va_t, qb_t, kb_t, vb_t), kv_s = _project_all(hs, w_segs, cos_s, sin_s)
    oa = _diff_sample(qa_t, ka_t, va_t, jnp.transpose(cache_diff_k[0], (0, 2, 3, 1)), cache_diff_v[0],
                      lam_vecs, g_diff_head, bs, ts, past)
    ob = _stick_sample(qb_t, kb_t, vb_t, cache_sb_k[0], cache_sb_v[0], bs, ts, past)
    ys = _tail(xs, hs, oa, ob, w).reshape(bs, ts, d)

    def caches(kv, b, t):
        ka_f, va_f, kb_f, vb_f = kv
        if ka_f.ndim == 4:
            ka_f = jnp.transpose(ka_f, (0, 3, 1, 2))
        return (ka_f.reshape(1, b, t, 2 * HA, HD_A), va_f.reshape(1, b, t, HA, VD_A),
                kb_f.reshape(1, b, t, HB, HD_B), vb_f.reshape(1, b, t, HB, HD_B))

    return (yp, ys) + caches(kv_p, bp, tp) + caches(kv_s, bs, ts)
```

```python
import functools
import math

import jax
import jax.numpy as jnp
from jax import lax
from jax.experimental import pallas as pl
from jax.experimental.pallas import tpu as pltpu

F32 = jnp.float32
BF16 = jnp.bfloat16

CHUNK = 64
HA = 8
HD_A = 64
VD_A = 128
HB = 8
HD_B = 128
SEG_W = 1024
LANES = 128
N_GROUPS = SEG_W // LANES
ROPE_THETA = 10000.0
EPS = 1e-6
NEG = -1e30
LAM_INIT = 0.8 - 0.6 * math.exp(-0.3 * 0)
UNDERFLOW_LOG = -104.0
DENOM_MIN = 1e-20
TRIP_BLOCKS = (16, 8, 4, 2)
ONES_ROWS = 16
VMEM_LIMIT = 56 * 1024 * 1024


def _tile(n, pref, mult=8):
    if n <= pref:
        return n
    for t in range(pref, 0, -1):
        if n % t == 0 and t % mult == 0:
            return t
    raise ValueError(f"no tile for {n}")


def _params(*sem):
    return pltpu.CompilerParams(dimension_semantics=sem, vmem_limit_bytes=VMEM_LIMIT)


def _dot(a, b):
    return jnp.dot(a, b, preferred_element_type=F32)


def _dot_nt(a, b):
    return lax.dot_general(a, b, (((1,), (1,)), ((), ())), preferred_element_type=F32)


def _rms(x, g):
    return x * lax.rsqrt(jnp.mean(x * x, axis=-1, keepdims=True) + EPS) * g


def _prenorm_q_kernel(x_ref, g_ref, w_ref, h_ref, t_ref):
    h = _rms(x_ref[...], g_ref[...]).astype(BF16)
    h_ref[...] = h
    for c0 in range(0, N_GROUPS, PROJ_CHUNK):
        _store_groups(t_ref, _dot(h, w_ref[:, c0 * LANES:(c0 + PROJ_CHUNK) * LANES]), c0)


def _prenorm_project_q(x, g, w):
    rows, d = x.shape
    tm = _tile(rows, 512)
    return pl.pallas_call(
        _prenorm_q_kernel,
        grid=(rows // tm,),
        in_specs=[pl.BlockSpec((tm, d), lambda i: (i, 0)), pl.BlockSpec((1, d), lambda i: (0, 0)),
                  pl.BlockSpec((d, SEG_W), lambda i: (0, 0))],
        out_specs=[pl.BlockSpec((tm, d), lambda i: (i, 0)), pl.BlockSpec((N_GROUPS, tm, LANES), lambda i: (0, i, 0))],
        out_shape=[jax.ShapeDtypeStruct((rows, d), BF16), jax.ShapeDtypeStruct((N_GROUPS, rows, LANES), BF16)],
        compiler_params=_params("parallel"),
        name="prenorm_proj_q",
    )(x, g, w)


PROJ_CHUNK = 2


def _project_chunks(h_ref, w_ref):
    h = h_ref[...]
    for c0 in range(0, N_GROUPS, PROJ_CHUNK):
        yield c0, _dot(h, w_ref[:, c0 * LANES:(c0 + PROJ_CHUNK) * LANES])


def _groups(z):
    return [z[:, c * LANES:(c + 1) * LANES] for c in range(z.shape[1] // LANES)]


def _store_groups(t_ref, z, c0):
    for c, zc in enumerate(_groups(z)):
        t_ref[c0 + c] = zc.astype(BF16)


def _rope_groups(z, cos, sin):
    lane = lax.broadcasted_iota(jnp.int32, (1, LANES), 1)
    first_half = (lane % HD_A) < (HD_A // 2)
    out = []
    for zc in _groups(z):
        partner = jnp.where(first_half, pltpu.roll(zc, LANES - HD_A // 2, 1), pltpu.roll(zc, HD_A // 2, 1))
        out.append(zc * cos + partner * sin)
    return out


def _store_heads(f_ref, z, c0):
    for c, zc in enumerate(_groups(z)):
        f_ref[:, c0 + c, :] = zc


def _proj_kv_kernel(h_ref, w_ref, f_ref, t_ref):
    for c0, z in _project_chunks(h_ref, w_ref):
        _store_heads(f_ref, z, c0)
        _store_groups(t_ref, z, c0)


def _proj_v_blocks_kernel(h_ref, w_ref, f_ref, t_ref, *, tk):
    for c0, z in _project_chunks(h_ref, w_ref):
        _store_heads(f_ref, z, c0)
        for c, zc in enumerate(_groups(z)):
            zt = zc.T
            for j in range(zt.shape[1] // tk):
                t_ref[c0 + c, j, :LANES, :] = zt[:, j * tk:(j + 1) * tk].astype(BF16)
    row = lax.broadcasted_iota(jnp.int32, (ONES_ROWS, tk), 0)
    ones_rows = jnp.where(row == 0, 1.0, 0.0).astype(BF16)
    for c in range(N_GROUPS):
        for j in range(t_ref.shape[1]):
            t_ref[c, j, LANES:, :] = ones_rows


def _proj_q_kernel(h_ref, w_ref, t_ref):
    for c0, z in _project_chunks(h_ref, w_ref):
        _store_groups(t_ref, z, c0)


def _proj_k_rope_kernel(h_ref, w_ref, cos_ref, sin_ref, f_ref, t_ref):
    for c0, z in _project_chunks(h_ref, w_ref):
        for c, zc in enumerate(_rope_groups(z, cos_ref[...], sin_ref[...]), c0):
            f_ref[:, c * LANES:(c + 1) * LANES] = zc
            t_ref[c] = zc.astype(BF16)


def _proj_k_rope_t_kernel(h_ref, w_ref, cos_ref, sin_ref, f_ref, t_ref):
    for c0, z in _project_chunks(h_ref, w_ref):
        for c, zc in enumerate(_rope_groups(z, cos_ref[...], sin_ref[...]), c0):
            zt = zc.T
            f_ref[0, 2 * c] = zt[:HD_A]
            f_ref[0, 2 * c + 1] = zt[HD_A:]
            t_ref[c] = zc.astype(BF16)


def _proj_q_rope_kernel(h_ref, w_ref, cos_ref, sin_ref, t_ref):
    for c0, z in _project_chunks(h_ref, w_ref):
        for c, zc in enumerate(_rope_groups(z, cos_ref[...], sin_ref[...]), c0):
            t_ref[c] = (zc * (HD_A ** -0.5)).astype(BF16)


def _project(h, w, mode, cos=None, sin=None, tk=None):
    rows, d = h.shape
    tm = _tile(rows, 1024)
    n_tab = None if cos is None else cos.shape[0] // tm
    in_specs = [pl.BlockSpec((tm, d), lambda i: (i, 0)), pl.BlockSpec((d, SEG_W), lambda i: (0, 0))]
    args = [h, w]
    if cos is not None:
        tab = pl.BlockSpec((tm, LANES), lambda i: (i % n_tab, 0))
        in_specs += [tab, tab]
        args += [cos, sin]
    t_spec = pl.BlockSpec((N_GROUPS, tm, LANES), lambda i: (0, i, 0))
    t_shape = jax.ShapeDtypeStruct((N_GROUPS, rows, LANES), BF16)
    with_f32 = mode in ("kv", "k_rope", "k_rope_t", "v_blocks")
    body = {"kv": _proj_kv_kernel, "q": _proj_q_kernel, "k_rope": _proj_k_rope_kernel,
            "k_rope_t": _proj_k_rope_t_kernel, "q_rope": _proj_q_rope_kernel,
            "v_blocks": functools.partial(_proj_v_blocks_kernel, tk=tk)}[mode]
    if mode == "k_rope":
        f_spec = pl.BlockSpec((tm, SEG_W), lambda i: (i, 0))
        f_shape = jax.ShapeDtypeStruct((rows, SEG_W), F32)
    elif mode == "k_rope_t":
        f_spec = pl.BlockSpec((1, 2 * HA, HD_A, tm), lambda i: (i // n_tab, 0, 0, i % n_tab))
        f_shape = jax.ShapeDtypeStruct((rows // tk, 2 * HA, HD_A, tk), F32)
    else:
        f_spec = pl.BlockSpec((tm, N_GROUPS, LANES), lambda i: (i, 0, 0))
        f_shape = jax.ShapeDtypeStruct((rows, N_GROUPS, LANES), F32)
    if mode == "v_blocks":
        t_spec = pl.BlockSpec((N_GROUPS, tm // tk, LANES + ONES_ROWS, tk), lambda i: (0, i, 0, 0))
        t_shape = jax.ShapeDtypeStruct((N_GROUPS, rows // tk, LANES + ONES_ROWS, tk), BF16)
    out = pl.pallas_call(
        body,
        grid=(rows // tm,),
        in_specs=in_specs,
        out_specs=[f_spec, t_spec] if with_f32 else t_spec,
        out_shape=[f_shape, t_shape] if with_f32 else t_shape,
        compiler_params=_params("parallel"),
        name="proj_" + mode,
    )(*args)
    return (out[0], out[1]) if with_f32 else (None, out)


def _rope_tables(pos):
    half = HD_A // 2
    inv = ROPE_THETA ** (-jnp.arange(half, dtype=F32) / half)
    ang = pos.astype(F32)[:, None] * inv[None, :]
    cos, sin = jnp.cos(ang), jnp.sin(ang)
    reps = LANES // HD_A
    return (jnp.tile(jnp.concatenate([cos, cos], axis=-1), (1, reps)),
            jnp.tile(jnp.concatenate([-sin, sin], axis=-1), (1, reps)))


def _head_lanes(q, c):
    lane = lax.broadcasted_iota(jnp.int32, (1, LANES), 1)
    return jnp.where((lane >= c * HD_A) & (lane < (c + 1) * HD_A), q, jnp.zeros_like(q))


def _suffix_matrix(n):
    j = lax.broadcasted_iota(jnp.int32, (2 * n, n), 0) % n
    s = lax.broadcasted_iota(jnp.int32, (2 * n, n), 1)
    return jnp.where(j >= s, 1.0, 0.0).astype(BF16)


def _log_keep_terms(z, mask):
    neg = -z
    log_keep = jnp.minimum(neg, 0.0) - jnp.log(1.0 + jnp.exp(jnp.minimum(z, neg)))
    if mask is not None:
        log_keep = jnp.where(mask, log_keep, 0.0)
    hi = log_keep.astype(BF16)
    return jnp.concatenate([hi, (log_keep - hi.astype(F32)).astype(BF16)], axis=1)


def _stick_block(z, mask, carry, v, acc_ref, suffix_mat):
    suffix = _dot(_log_keep_terms(z, mask), suffix_mat)
    a = jnp.exp(z + suffix + carry)
    if mask is not None:
        a = jnp.where(mask, a, 0.0)
    acc_ref[...] += _dot(a.astype(BF16), v)
    return carry + suffix[:, 0:1]


def _lambda(lam_ref):
    lam = lam_ref[...]
    d1 = jnp.sum(lam[0:1] * lam[1:2], axis=-1, keepdims=True)
    d2 = jnp.sum(lam[2:3] * lam[3:4], axis=-1, keepdims=True)
    return jnp.exp(d1) - jnp.exp(d2) + LAM_INIT


def _diff_finish(o1, o2, lam, g_head):
    o = o1 - lam * o2
    o = o * lax.rsqrt(jnp.mean(o * o, axis=-1, keepdims=True) + EPS)
    return (o * g_head * (1.0 - LAM_INIT)).astype(BF16)


def _key_norm_bound(k_ref, c, seq, tq):
    rows = _tile(seq, 2048, 16)
    lane = lax.broadcasted_iota(jnp.int32, (8, LANES), 1)
    ones_c = jnp.where((lane >= c * HD_A) & (lane < (c + 1) * HD_A), 1.0, 0.0).astype(BF16)

    def body(i, best):
        k = k_ref[0, pl.ds(pl.multiple_of(i * rows, rows), rows), :].astype(F32)
        return jnp.maximum(best, _dot_nt(ones_c, (k * k).astype(BF16)))

    best = lax.fori_loop(0, seq // rows, body, jnp.zeros((8, rows), F32))
    return jnp.broadcast_to(jnp.max(best[0:1], axis=1, keepdims=True), (1, tq))


def _diff_prompt_kernel(q1_ref, q2_ref, k1_ref, k2_ref, vt_ref, lam_ref, g_ref, o_ref, acc_ref, s_ref, p_ref,
                        kmax_ref, *, tq, tk, seq):
    qi = pl.program_id(2)
    lam = _lambda(lam_ref)

    @pl.when(qi == 0)
    def _():
        for idx, (c, k_ref) in enumerate((c, k_ref) for c in range(2) for k_ref in (k1_ref, k2_ref)):
            kmax_ref[idx] = _key_norm_bound(k_ref, c, seq, tq)

    n_full = (qi * tq) // tk
    n_edge = tq // tk
    q_pos = qi * tq + lax.broadcasted_iota(jnp.int32, (1, tq), 1)
    limit = (q_pos // CHUNK + 1) * CHUNK
    k_pos = n_full * tk + lax.broadcasted_iota(jnp.int32, (tk, 1), 0)
    edge_masks = [k_pos + e * tk < limit for e in range(n_edge)]

    streams = [(c, _head_lanes(q_ref[0], c), k_ref)
               for c in range(2) for q_ref, k_ref in ((q1_ref, k1_ref), (q2_ref, k2_ref))]
    n_streams = len(streams)

    def score_block(j):
        start = pl.multiple_of(j * tk, tk)
        return [_dot_nt(k_ref[0, pl.ds(start, tk), :], q) for _, q, k_ref in streams]

    def after(x):
        return 0.0 * x

    def store_scores(scores, readers=None):
        for idx, s in enumerate(scores):
            s_ref[idx] = s if readers is None else s + readers[idx]

    def value_block(j, alphas, slot=1):
        done = []
        for idx, (c, _, _) in enumerate(streams):
            pv = _dot(vt_ref[c, j], p_ref[slot, idx])
            acc_ref[idx] = pv + (acc_ref[idx] if alphas is None else alphas[idx] * acc_ref[idx])
            done.append(after(pv[VD_A:VD_A + 1, :]))
        return done

    def masked(s, mask):
        return s if mask is None else jnp.where(mask, s, NEG)

    ones8 = jnp.ones((8, LANES), BF16)
    bounds = []
    for idx, (_, q, _) in enumerate(streams):
        qf = q.astype(F32)
        q_sq = _dot_nt(ones8, (qf * qf).astype(BF16))[0:1]
        bounds.append(jnp.sqrt(q_sq * kmax_ref[idx]) * 1.02)

    def store_weights(scores, mask, slot, readers=None):
        for idx, s in enumerate(scores):
            shift = bounds[idx] if readers is None else bounds[idx] + readers[idx]
            p_ref[slot, idx] = jnp.exp(masked(s, mask) - shift).astype(BF16)

    def bound_block(j, mask):
        scores = score_block(j)
        done = value_block(jnp.maximum(j - 1, 0), None)
        store_weights(scores, mask, 1, done)

    def bound_blocks(j, masks):
        n = len(masks)
        scores = score_block(j)
        done = {1: value_block(jnp.maximum(j - 1, 0), None), 0: None}
        for u, mask in enumerate(masks):
            slot = u % 2
            next_scores = score_block(j + u + 1) if u < n - 1 else None
            store_weights(scores, mask, slot, done[slot])
            if u < n - 1:
                done[slot] = value_block(j + u, None, slot=slot)
            scores = next_scores

    acc_ref[...] = jnp.zeros_like(acc_ref)
    p_ref[1] = jnp.zeros_like(p_ref[1])
    first = 0
    for n in TRIP_BLOCKS:
        trips = (n_full - first) // n
        lax.fori_loop(first // n, first // n + trips, lambda i, _, n=n: bound_blocks(n * i, [None] * n), None)
        first = first + trips * n

    @pl.when(n_full % 2 == 1)
    def _():
        bound_block(n_full - 1, None)

    if n_edge % 2 == 0:
        bound_blocks(n_full, edge_masks)
    else:
        for e, mask in enumerate(edge_masks):
            bound_block(n_full + e, mask)
    value_block(n_full + n_edge - 1, None)

    denom_min = acc_ref[0, VD_A:VD_A + 1, :]
    for idx in range(1, n_streams):
        denom_min = jnp.minimum(denom_min, acc_ref[idx, VD_A:VD_A + 1, :])

    @pl.when(jnp.logical_not(jnp.min(denom_min) >= DENOM_MIN))
    def _():
        def max_block(maxima, mask, done):
            new_maxima, alphas = [], []
            for idx in range(n_streams):
                s = masked(s_ref[idx], mask)
                m_new = jnp.maximum(maxima[idx], jnp.max(s, axis=0, keepdims=True))
                alphas.append(jnp.exp(maxima[idx] - m_new))
                new_maxima.append(m_new)
                p_ref[1, idx] = jnp.exp(s - (m_new + done[idx])).astype(BF16)
            return tuple(new_maxima), tuple(alphas)

        def step(j, carry):
            maxima, alphas = carry
            next_scores = score_block(j + 1)
            done = value_block(jnp.maximum(j - 1, 0), alphas)
            maxima, alphas = max_block(maxima, None, done)
            store_scores(next_scores, [after(m) for m in maxima])
            return maxima, alphas

        acc_ref[...] = jnp.zeros_like(acc_ref)
        p_ref[...] = jnp.zeros_like(p_ref)
        store_scores(score_block(0))
        maxima = tuple(jnp.full((1, tq), NEG, F32) for _ in streams)
        alphas = tuple(jnp.ones((1, tq), F32) for _ in streams)
        maxima, alphas = lax.fori_loop(0, n_full, step, (maxima, alphas))
        done = value_block(jnp.maximum(n_full - 1, 0), alphas)
        for e, mask in enumerate(edge_masks):
            if e > 0:
                store_scores(score_block(n_full + e), [after(m) for m in maxima])
            maxima, alphas = max_block(maxima, mask, done)
            done = value_block(n_full + e, alphas)

    for c in range(2):
        a1, a2 = acc_ref[2 * c], acc_ref[2 * c + 1]
        o = a1[:VD_A] / a1[VD_A:VD_A + 1] - lam * (a2[:VD_A] / a2[VD_A:VD_A + 1])
        o = o * lax.rsqrt(jnp.mean(o * o, axis=0, keepdims=True) + EPS)
        o = o * g_ref[...] * (1.0 - LAM_INIT)
        o_ref[:, c * VD_A:(c + 1) * VD_A] = o.T.astype(BF16)


def _diff_prompt(qa_t, ka_t, va_tt, lam_vecs, g_head, batch, seq, tk):
    tq = 2 * tk if seq % (2 * tk) == 0 else tk
    nq = seq // tq
    half = N_GROUPS // 2
    q_spec = lambda off: pl.BlockSpec((1, tq, LANES), lambda b, p, i: (p + off, b * nq + i, 0))
    k_spec = lambda off: pl.BlockSpec((1, seq, LANES), lambda b, p, i: (p + off, b, 0))
    return pl.pallas_call(
        functools.partial(_diff_prompt_kernel, tq=tq, tk=tk, seq=seq),
        grid=(batch, half, nq),
        in_specs=[q_spec(0), q_spec(half), k_spec(0), k_spec(half),
                  pl.BlockSpec((2, seq // tk, VD_A + ONES_ROWS, tk), lambda b, p, i: (p, b, 0, 0)),
                  pl.BlockSpec((4, HD_A), lambda b, p, i: (0, 0)),
                  pl.BlockSpec((VD_A, 1), lambda b, p, i: (0, 0))],
        out_specs=pl.BlockSpec((tq, 2 * VD_A), lambda b, p, i: (b * nq + i, p)),
        out_shape=jax.ShapeDtypeStruct((batch * seq, SEG_W), BF16),
        scratch_shapes=[pltpu.VMEM((4, VD_A + ONES_ROWS, tq), F32), pltpu.VMEM((4, tk, tq), F32),
                        pltpu.VMEM((2, 4, tk, tq), BF16), pltpu.VMEM((4, 1, tq), F32)],
        compiler_params=_params("arbitrary", "arbitrary", "arbitrary"),
        name="diff_prompt",
    )(qa_t, qa_t, ka_t, ka_t, va_tt, lam_vecs, g_head.reshape(VD_A, 1))


def _stick_blocks(zs, mask, carries, vs, acc_ref, suffix_mat):
    terms = [_log_keep_terms(z, mask) for z in zs]
    suffixes = [_dot(t, suffix_mat) for t in terms]
    weights = []
    for z, suffix, carry in zip(zs, suffixes, carries):
        a = jnp.exp(z + suffix + carry)
        if mask is not None:
            a = jnp.where(mask, a, 0.0)
        weights.append(a.astype(BF16))
    for idx, (a, v) in enumerate(zip(weights, vs)):
        acc_ref[idx] += _dot(a, v)
    return tuple(carry + suffix[:, 0:1] for carry, suffix in zip(carries, suffixes))


def _stick_prompt_kernel(q_ref, k_ref, v_ref, o_ref, acc_ref, *, tq, tk, heads):
    qi = pl.program_id(2)
    qs = [q_ref[h] for h in range(heads)]
    suffix_mat = _suffix_matrix(tk)
    scale = HD_B ** -0.5
    acc_ref[...] = jnp.zeros_like(acc_ref)
    q_pos = qi * tq + lax.broadcasted_iota(jnp.int32, (tq, 1), 0)

    def block(start, mask, carries):
        zs = [_dot_nt(qs[h], k_ref[h, pl.ds(start, tk), :]) * scale for h in range(heads)]
        vs = [v_ref[h, pl.ds(start, tk), :] for h in range(heads)]
        return _stick_blocks(zs, mask, carries, vs, acc_ref, suffix_mat)

    carries = tuple(jnp.zeros((tq, 1), F32) for _ in range(heads))
    for e in range(tq // tk - 1, -1, -1):
        start = pl.multiple_of(qi * tq + e * tk, tk)
        k_pos = start + lax.broadcasted_iota(jnp.int32, (1, tk), 1)
        carries = block(start, k_pos < q_pos, carries)

    n_full = (qi * tq) // tk

    def cond(st):
        i, carries = st
        largest = functools.reduce(jnp.maximum, [jnp.max(c) for c in carries])
        return jnp.logical_and(i < n_full, largest > UNDERFLOW_LOG)

    def body(st):
        i, carries = st
        return i + 1, block(pl.multiple_of((n_full - 1 - i) * tk, tk), None, carries)

    lax.while_loop(cond, body, (jnp.int32(0), carries))
    for h in range(heads):
        o_ref[:, h * HD_B:(h + 1) * HD_B] = acc_ref[h].astype(BF16)


def _stick_prompt(qb_t, kb_t, vb_t, batch, seq):
    tq = _tile(seq, 256, 8)
    tk = tq
    heads = 4
    nq = seq // tq
    kv_spec = pl.BlockSpec((heads, seq, LANES), lambda b, h, i: (h, b, 0), pipeline_mode=pl.Buffered(1))
    return pl.pallas_call(
        functools.partial(_stick_prompt_kernel, tq=tq, tk=tk, heads=heads),
        grid=(batch, HB // heads, nq),
        in_specs=[pl.BlockSpec((heads, tq, LANES), lambda b, h, i: (h, b * nq + i, 0)), kv_spec, kv_spec],
        out_specs=pl.BlockSpec((tq, heads * HD_B), lambda b, h, i: (b * nq + i, h)),
        out_shape=jax.ShapeDtypeStruct((batch * seq, SEG_W), BF16),
        scratch_shapes=[pltpu.VMEM((heads, tq, HD_B), F32)],
        compiler_params=_params("parallel", "parallel", "arbitrary"),
        name="stick_prompt",
    )(qb_t, kb_t, vb_t)


def _diff_sample_kernel(qa_ref, ka_ref, va_ref, kc_ref, vc_ref, lam_ref, g_ref, o_ref, *, ts, past):
    lam = _lambda(lam_ref)
    q_pos = past + lax.broadcasted_iota(jnp.int32, (ts, 1), 0)
    limit = (q_pos // CHUNK + 1) * CHUNK
    k_pos = past + lax.broadcasted_iota(jnp.int32, (1, ts), 1)
    new_mask = k_pos < limit

    def attend(g, c, v_cache, v_new):
        q = _head_lanes(qa_ref[g], c)
        k_cache = jnp.concatenate([kc_ref[0, 2 * g].astype(BF16), kc_ref[0, 2 * g + 1].astype(BF16)], axis=0)
        s_cache = _dot(q, k_cache)
        s_new = jnp.where(new_mask, _dot_nt(q, ka_ref[g]), NEG)
        m = jnp.maximum(jnp.max(s_cache, axis=-1, keepdims=True), jnp.max(s_new, axis=-1, keepdims=True))
        p_cache = jnp.exp(s_cache - m)
        p_new = jnp.exp(s_new - m)
        l = jnp.sum(p_cache, axis=-1, keepdims=True) + jnp.sum(p_new, axis=-1, keepdims=True)
        return (_dot(p_cache.astype(BF16), v_cache) + _dot(p_new.astype(BF16), v_new)) / l

    half = N_GROUPS // 2
    for h in range(HA):
        g, c = h // 2, h % 2
        v_cache = vc_ref[0, :, h, :].astype(BF16)
        o1 = attend(g, c, v_cache, va_ref[h])
        o2 = attend(g + half, c, v_cache, va_ref[h])
        o_ref[:, h * VD_A:(h + 1) * VD_A] = _diff_finish(o1, o2, lam, g_ref[...])


def _diff_sample(qa_t, ka_t, va_t, cache_k, cache_v, lam_vecs, g_head, batch, ts, past):
    new_spec = pl.BlockSpec((N_GROUPS, ts, LANES), lambda b: (0, b, 0))
    return pl.pallas_call(
        functools.partial(_diff_sample_kernel, ts=ts, past=past),
        grid=(batch,),
        in_specs=[new_spec, new_spec, new_spec,
                  pl.BlockSpec((1, 2 * HA, HD_A, past), lambda b: (b, 0, 0, 0)),
                  pl.BlockSpec((1, past, HA, VD_A), lambda b: (b, 0, 0, 0)),
                  pl.BlockSpec((4, HD_A), lambda b: (0, 0)),
                  pl.BlockSpec((1, VD_A), lambda b: (0, 0))],
        out_specs=pl.BlockSpec((ts, SEG_W), lambda b: (b, 0)),
        out_shape=jax.ShapeDtypeStruct((batch * ts, SEG_W), BF16),
        compiler_params=_params("parallel"),
        name="diff_sample",
    )(qa_t, ka_t, va_t, cache_k, cache_v, lam_vecs, g_head)


def _stick_sample_kernel(q_ref, kn_ref, vn_ref, kc_ref, vc_ref, o_ref, acc_ref, *, ts, past, tk):
    scale = HD_B ** -0.5
    row = lax.broadcasted_iota(jnp.int32, (ts, 1), 0)
    col = lax.broadcasted_iota(jnp.int32, (1, ts), 1)
    new_suffix_mat = _suffix_matrix(ts)
    suffix_mat = _suffix_matrix(tk)
    n_full = past // tk

    for h in range(HB):
        q = q_ref[h]
        acc_ref[...] = jnp.zeros_like(acc_ref)
        z = _dot_nt(q, kn_ref[h]) * scale
        carry = _stick_block(z, col < row, jnp.zeros((ts, 1), F32), vn_ref[h], acc_ref, new_suffix_mat)

        def cond(st):
            i, carry = st
            return jnp.logical_and(i < n_full, jnp.max(carry) > UNDERFLOW_LOG)

        def body(st, q=q, h=h):
            i, carry = st
            start = pl.multiple_of((n_full - 1 - i) * tk, tk)
            k = kc_ref[0, pl.ds(start, tk), h, :].astype(BF16)
            v = vc_ref[0, pl.ds(start, tk), h, :].astype(BF16)
            return i + 1, _stick_block(_dot_nt(q, k) * scale, None, carry, v, acc_ref, suffix_mat)

        lax.while_loop(cond, body, (jnp.int32(0), carry))
        o_ref[:, h * HD_B:(h + 1) * HD_B] = acc_ref[...].astype(BF16)


def _stick_sample(qb_t, kb_t, vb_t, cache_k, cache_v, batch, ts, past):
    tk = _tile(past, 256, 8)
    new_spec = pl.BlockSpec((N_GROUPS, ts, LANES), lambda b: (0, b, 0))
    cache_spec = pl.BlockSpec((1, past, HB, HD_B), lambda b: (b, 0, 0, 0))
    return pl.pallas_call(
        functools.partial(_stick_sample_kernel, ts=ts, past=past, tk=tk),
        grid=(batch,),
        in_specs=[new_spec, new_spec, new_spec, cache_spec, cache_spec],
        out_specs=pl.BlockSpec((ts, SEG_W), lambda b: (b, 0)),
        out_shape=jax.ShapeDtypeStruct((batch * ts, SEG_W), BF16),
        scratch_shapes=[pltpu.VMEM((ts, HD_B), F32)],
        compiler_params=_params("parallel"),
        name="stick_sample",
    )(qb_t, kb_t, vb_t, cache_k, cache_v)


def _merge_kernel(h_ref, oa_ref, ob_ref, wga_ref, wgb_ref, wpa_ref, wpb_ref, o_ref):
    h = h_ref[...]
    ga = jax.nn.sigmoid(_dot(h, wga_ref[...]))
    gb = jax.nn.sigmoid(_dot(h, wgb_ref[...]))
    ya = _dot(oa_ref[...], wpa_ref[...])
    yb = _dot(ob_ref[...], wpb_ref[...])
    o_ref[...] = (ga * ya + gb * yb).astype(BF16)


def _merge(h, oa, ob, w_gate, w_proj_a, w_proj_b):
    rows, d = h.shape
    tm = _tile(rows, 1024)
    tn = 512
    nn = d // tn
    row_spec = lambda w: pl.BlockSpec((tm, w), lambda i, n: (i, 0))
    return pl.pallas_call(
        _merge_kernel,
        grid=(rows // tm, nn),
        in_specs=[row_spec(d), row_spec(SEG_W), row_spec(SEG_W),
                  pl.BlockSpec((d, tn), lambda i, n: (0, n)),
                  pl.BlockSpec((d, tn), lambda i, n: (0, nn + n)),
                  pl.BlockSpec((SEG_W, tn), lambda i, n: (0, n)),
                  pl.BlockSpec((SEG_W, tn), lambda i, n: (0, n))],
        out_specs=pl.BlockSpec((tm, tn), lambda i, n: (i, n)),
        out_shape=jax.ShapeDtypeStruct((rows, d), BF16),
        compiler_params=_params("parallel", "arbitrary"),
        name="merge",
    )(h, oa, ob, w_gate, w_gate, w_proj_a, w_proj_b)


def _out_kernel(x_ref, m_ref, w_ref, g_post_ref, g_pre_ref, x1_ref, hm_ref):
    x1 = x_ref[...] + _rms(_dot(m_ref[...], w_ref[...]), g_post_ref[...])
    x1_ref[...] = x1
    hm_ref[...] = _rms(x1, g_pre_ref[...]).astype(BF16)


def _out_proj(x, merged, w_out, g_post_mix, g_pre_mlp):
    rows, d = x.shape
    tm = _tile(rows, 512)
    row = pl.BlockSpec((tm, d), lambda i: (i, 0))
    vec = pl.BlockSpec((1, d), lambda i: (0, 0))
    return pl.pallas_call(
        _out_kernel,
        grid=(rows // tm,),
        in_specs=[row, row, pl.BlockSpec((d, d), lambda i: (0, 0)), vec, vec],
        out_specs=[row, row],
        out_shape=[jax.ShapeDtypeStruct((rows, d), F32), jax.ShapeDtypeStruct((rows, d), BF16)],
        compiler_params=_params("parallel"),
        name="out_proj",
    )(x, merged, w_out, g_post_mix, g_pre_mlp)


def _mlp_kernel(x1_ref, hm_ref, wu_ref, wd_ref, g_ref, y_ref, acc_ref):
    f = pl.program_id(1)

    @pl.when(f == 0)
    def _():
        acc_ref[...] = jnp.zeros_like(acc_ref)

    hm = hm_ref[...]
    half = wu_ref.shape[1] // 2
    down = []
    for s in range(2):
        u = jnp.square(jnp.maximum(_dot(hm, wu_ref[:, s * half:(s + 1) * half]), 0.0))
        down.append(_dot(u.astype(BF16), wd_ref[s * half:(s + 1) * half, :]))
    acc_ref[...] += down[0] + down[1]

    @pl.when(f == pl.num_programs(1) - 1)
    def _():
        y_ref[...] = x1_ref[...] + _rms(acc_ref[...], g_ref[...])


def _mlp(x1, hm, w_up, w_down, g_post_mlp):
    rows, d = x1.shape
    d_ff = w_up.shape[1]
    tm = _tile(rows, 512)
    tf = 1024
    row = pl.BlockSpec((tm, d), lambda i, f: (i, 0))
    return pl.pallas_call(
        _mlp_kernel,
        grid=(rows // tm, d_ff // tf),
        in_specs=[row, row,
                  pl.BlockSpec((d, tf), lambda i, f: (0, f)),
                  pl.BlockSpec((tf, d), lambda i, f: (f, 0)),
                  pl.BlockSpec((1, d), lambda i, f: (0, 0))],
        out_specs=row,
        out_shape=jax.ShapeDtypeStruct((rows, d), F32),
        scratch_shapes=[pltpu.VMEM((tm, d), F32)],
        compiler_params=_params("parallel", "arbitrary"),
        name="mlp",
    )(x1, hm, w_up, w_down, g_post_mlp)


def _project_all(x, g, w_segs, cos, sin, v_block=None):
    h, qb_t = _prenorm_project_q(x, g, w_segs[3])
    _, qa_t = _project(h, w_segs[0], "q_rope", cos, sin)
    if v_block is None:
        ka_f, ka_t = _project(h, w_segs[1], "k_rope", cos, sin)
    else:
        ka_f, ka_t = _project(h, w_segs[1], "k_rope_t", cos, sin, tk=cos.shape[0])
    if v_block is None:
        va_f, va_t = _project(h, w_segs[2], "kv")
    else:
        va_f, va_t = _project(h, w_segs[2], "v_blocks", tk=v_block)
    kb_f, kb_t = _project(h, w_segs[4], "kv")
    vb_f, vb_t = _project(h, w_segs[5], "kv")
    return h, (qa_t, ka_t, va_t, qb_t, kb_t, vb_t), (ka_f, va_f, kb_f, vb_f)


def _tail(x, h, oa, ob, w):
    merged = _merge(h, oa, ob, w["gate"], w["proj_a"], w["proj_b"])
    x1, hm = _out_proj(x, merged, w["out"], w["g_post_mix"], w["g_pre_mlp"])
    return _mlp(x1, hm, w["up"], w["down"], w["g_post_mlp"])


def kernel(x_prompt, x_sample, cache_diff_k, cache_diff_v, cache_sb_k, cache_sb_v, g_pre_mix, w_in, lambda_q1, lambda_k1, lambda_q2, lambda_k2, g_diff_head, w_gate, w_proj_a, w_proj_b, w_out, g_post_mix, g_pre_mlp, w_up, w_down, g_post_mlp):
    bp, tp, d = x_prompt.shape
    bs, ts, _ = x_sample.shape
    depth, _, past = cache_diff_k.shape[:3]
    assert depth == 1 and tp % CHUNK == 0

    w_in_b = w_in[0].astype(BF16)
    w_segs = [w_in_b[:, s * SEG_W:(s + 1) * SEG_W] for s in range(6)]
    w = dict(gate=w_gate[0].astype(BF16), proj_a=w_proj_a[0].astype(BF16), proj_b=w_proj_b[0].astype(BF16),
             out=w_out[0].astype(BF16), up=w_up[0].astype(BF16), down=w_down[0].astype(BF16),
             g_post_mix=g_post_mix, g_pre_mlp=g_pre_mlp, g_post_mlp=g_post_mlp)
    lam_vecs = jnp.concatenate([lambda_q1, lambda_k1, lambda_q2, lambda_k2], axis=0).astype(F32)

    xp = x_prompt.reshape(bp * tp, d)
    cos_p, sin_p = _rope_tables(jnp.arange(tp, dtype=jnp.int32))
    tk_diff = _tile(tp, 256, CHUNK)
    hp, (qa_t, ka_t, va_tt, qb_t, kb_t, vb_t), kv_p = _project_all(xp, g_pre_mix, w_segs, cos_p, sin_p, v_block=tk_diff)
    oa = _diff_prompt(qa_t, ka_t, va_tt, lam_vecs, g_diff_head, bp, tp, tk_diff)
    ob = _stick_prompt(qb_t, kb_t, vb_t, bp, tp)
    yp = _tail(xp, hp, oa, ob, w).reshape(bp, tp, d)

    xs = x_sample.reshape(bs * ts, d)
    pos_s = past + jnp.arange(ts, dtype=jnp.int32)
    cos_s, sin_s = _rope_tables(jnp.tile(pos_s, bs))
    hs, (qa_t, ka_t, va_t, qb_t, kb_t, vb_t), kv_s = _project_all(xs, g_pre_mix, w_segs, cos_s, sin_s)
    oa = _diff_sample(qa_t, ka_t, va_t, jnp.transpose(cache_diff_k[0], (0, 2, 3, 1)), cache_diff_v[0],
                      lam_vecs, g_diff_head, bs, ts, past)
    ob = _stick_sample(qb_t, kb_t, vb_t, cache_sb_k[0], cache_sb_v[0], bs, ts, past)
    ys = _tail(xs, hs, oa, ob, w).reshape(bs, ts, d)

    def caches(kv, b, t):
        ka_f, va_f, kb_f, vb_f = kv
        if ka_f.ndim == 4:
            ka_f = jnp.transpose(ka_f, (0, 3, 1, 2))
        return (ka_f.reshape(1, b, t, 2 * HA, HD_A), va_f.reshape(1, b, t, HA, VD_A),
                kb_f.reshape(1, b, t, HB, HD_B), vb_f.reshape(1, b, t, HB, HD_B))

    return (yp, ys) + caches(kv_p, bp, tp) + caches(kv_s, bs, ts)
```
